```python
import jax, jax.numpy as jnp
from jax import lax
import numpy as np

D_MODEL = 4096
BATCH = 2
SEQ = 4096
DEPTH = 1
DEC_BATCH = 16
DEC_SEQ = 64
PAST_LEN = 1024

CHUNK = 64
GMLP_CHUNK = 2 * CHUNK
D_MIX = D_MODEL
C_CONV = 3 * D_MIX // 8
C_GMLP = 3 * D_MIX // 8
C_X = D_MIX - C_CONV - C_GMLP
CONV_W = 31
G_HEADS = 4
G_HEAD_DIM = C_GMLP // G_HEADS
X_HEADS = 4
X_HEAD_DIM = C_X // X_HEADS
N_MEM = 256
N_IN = 3 * C_CONV + 3 * C_GMLP + 2 * C_X
SPLITS = [int(s) for s in np.cumsum([C_CONV, C_CONV, C_CONV, C_GMLP, C_GMLP, C_GMLP, C_X])]
EPS = 1e-6

kernel_name = 'hymba_conformer_gmlp_memxattn_stream_step'


def rms_norm(x, g):
    xf = x.astype(jnp.float32)
    y = xf * lax.rsqrt(jnp.mean(xf * xf, axis=-1, keepdims=True) + EPS)
    return (y * g.astype(jnp.float32)).astype(x.dtype)


def layer_norm(x, g, b):
    xf = x.astype(jnp.float32)
    xc = xf - jnp.mean(xf, axis=-1, keepdims=True)
    y = xc * lax.rsqrt(jnp.mean(xc * xc, axis=-1, keepdims=True) + EPS)
    return (y * g.astype(jnp.float32) + b.astype(jnp.float32)).astype(x.dtype)


def causal_dwconv(hist, w, b):
    out = lax.conv_general_dilated(hist, w[:, None, :], window_strides=(1,), padding='VALID',
                                   dimension_numbers=('NWC', 'WIO', 'NWC'),
                                   feature_group_count=C_CONV)
    return out + b


def spatial_gate(vn, w_s, b_s):
    B, T, _ = vn.shape
    n = -(-T // GMLP_CHUNK)
    vp = jnp.pad(vn, ((0, 0), (0, n * GMLP_CHUNK - T), (0, 0)))
    vp = vp.reshape(B, n, GMLP_CHUNK, G_HEADS, G_HEAD_DIM)
    tri = jnp.tril(jnp.ones((GMLP_CHUNK, GMLP_CHUNK), dtype=bool))
    wm = jnp.where(tri[None], w_s, 0).astype(vp.dtype)
    z = jnp.einsum('hij,bnjhd->bnihd', wm, vp) + b_s.T[None, None, :, :, None]
    return z.reshape(B, n * GMLP_CHUNK, C_GMLP)[:, :T]


def memory_kv(mem, g_mem, w_mk, w_mv):
    B = mem.shape[0]
    m = rms_norm(mem, g_mem)
    k = (m @ w_mk).reshape(B, N_MEM, X_HEADS, X_HEAD_DIM)
    v = (m @ w_mv).reshape(B, N_MEM, X_HEADS, X_HEAD_DIM)
    return k, v


def cross_attend(q, mem_k, mem_v):
    B, T, _ = q.shape
    qh = q.reshape(B, T, X_HEADS, X_HEAD_DIM) * (X_HEAD_DIM ** -0.5)
    s = jnp.einsum('bthd,bmhd->bhtm', qh, mem_k).astype(jnp.float32)
    p = jax.nn.softmax(s, axis=-1).astype(mem_v.dtype)
    return jnp.einsum('bhtm,bmhd->bthd', p, mem_v).reshape(B, T, C_X)


def mixer_layer(x, conv_prev, mem_k, mem_v, g_pre, w_in, conv_w, conv_b, ln_conv_g, ln_conv_b,
                ln_v_g, ln_v_b, w_spatial, b_spatial, g_branch, w_out, g_post):
    h = rms_norm(x, g_pre)
    p = h @ w_in
    a, bg, gc, u, v, gg, q, gx = jnp.split(p, SPLITS, axis=-1)
    glu = a * jax.nn.sigmoid(bg)
    hist = jnp.concatenate([conv_prev, glu], axis=1)
    c = causal_dwconv(hist, conv_w, conv_b)
    yc = jax.nn.silu(layer_norm(c, ln_conv_g, ln_conv_b)) * jax.nn.silu(gc)
    new_conv = hist[:, -(CONV_W - 1):]
    vn = layer_norm(v, ln_v_g, ln_v_b)
    yg = u * spatial_gate(vn, w_spatial, b_spatial) * jax.nn.silu(gg)
    yx = cross_attend(q, mem_k, mem_v) * jax.nn.silu(gx)
    mixed = jnp.concatenate([
        rms_norm(yc, g_branch[:C_CONV]),
        rms_norm(yg, g_branch[C_CONV:C_CONV + C_GMLP]),
        rms_norm(yx, g_branch[C_CONV + C_GMLP:]),
    ], axis=-1)
    out = mixed @ w_out
    return x + rms_norm(out, g_post), new_conv, vn


def setup_inputs(seed: int = 0) -> dict:
    key = jax.random.key(seed)
    ks = jax.random.split(key, 32)
    f32 = jnp.float32
    nrm = lambda k, shape, s=1.0: (s * jax.random.normal(k, shape)).astype(f32)
    gain = lambda k, shape: (1.0 + 0.02 * jax.random.normal(k, shape)).astype(f32)
    return {
        'x_prompt': nrm(ks[0], (BATCH, SEQ, D_MODEL)),
        'x_sample': nrm(ks[1], (DEC_BATCH, DEC_SEQ, D_MODEL)),
        'mem_prompt': nrm(ks[2], (BATCH, N_MEM, D_MODEL)),
        'cache_mem_k': nrm(ks[3], (DEPTH, DEC_BATCH, N_MEM, X_HEADS, X_HEAD_DIM)),
        'cache_mem_v': nrm(ks[4], (DEPTH, DEC_BATCH, N_MEM, X_HEADS, X_HEAD_DIM)),
        'cache_conv': nrm(ks[5], (DEPTH, DEC_BATCH, CONV_W - 1, C_CONV), 0.5),
        'g_pre': gain(ks[6], (DEPTH, D_MODEL)),
        'w_in': nrm(ks[7], (DEPTH, D_MODEL, N_IN), D_MODEL ** -0.5),
        'conv_w': nrm(ks[8], (DEPTH, CONV_W, C_CONV), CONV_W ** -0.5),
        'conv_b': nrm(ks[9], (DEPTH, C_CONV), 0.02),
        'ln_conv_g': gain(ks[10], (DEPTH, C_CONV)),
        'ln_conv_b': nrm(ks[11], (DEPTH, C_CONV), 0.02),
        'ln_v_g': gain(ks[12], (DEPTH, C_GMLP)),
        'ln_v_b': nrm(ks[13], (DEPTH, C_GMLP), 0.02),
        'w_spatial': nrm(ks[14], (DEPTH, G_HEADS, GMLP_CHUNK, GMLP_CHUNK), GMLP_CHUNK ** -0.5),
        'b_spatial': (1.0 + 0.1 * jax.random.normal(ks[15], (DEPTH, G_HEADS, GMLP_CHUNK))).astype(f32),
        'g_mem': gain(ks[16], (DEPTH, D_MODEL)),
        'w_mk': nrm(ks[17], (DEPTH, D_MODEL, C_X), D_MODEL ** -0.5),
        'w_mv': nrm(ks[18], (DEPTH, D_MODEL, C_X), D_MODEL ** -0.5),
        'g_branch': gain(ks[19], (DEPTH, D_MIX)),
        'w_out': nrm(ks[20], (DEPTH, D_MIX, D_MODEL), D_MIX ** -0.5),
        'g_post': gain(ks[21], (DEPTH, D_MODEL)),
    }


def reference(x_prompt, x_sample, mem_prompt, cache_mem_k, cache_mem_v, cache_conv,
              g_pre, w_in, conv_w, conv_b, ln_conv_g, ln_conv_b, ln_v_g, ln_v_b,
              w_spatial, b_spatial, g_mem, w_mk, w_mv, g_branch, w_out, g_post):
    yp, ys = x_prompt, x_sample
    mk_l, mv_l, cp_l, cs_l, gv_l = [], [], [], [], []
    for l in range(DEPTH):
        lw = (g_pre[l], w_in[l], conv_w[l], conv_b[l], ln_conv_g[l], ln_conv_b[l],
              ln_v_g[l], ln_v_b[l], w_spatial[l], b_spatial[l], g_branch[l], w_out[l], g_post[l])
        mk, mv = memory_kv(mem_prompt, g_mem[l], w_mk[l], w_mv[l])
        conv0 = jnp.zeros((yp.shape[0], CONV_W - 1, C_CONV), dtype=yp.dtype)
        yp, cp, _ = mixer_layer(yp, conv0, mk, mv, *lw)
        ys, cs, gv = mixer_layer(ys, cache_conv[l], cache_mem_k[l], cache_mem_v[l], *lw)
        mk_l.append(mk); mv_l.append(mv); cp_l.append(cp); cs_l.append(cs); gv_l.append(gv)
    mem_k_prompt = jnp.stack(mk_l)
    mem_v_prompt = jnp.stack(mv_l)
    conv_prompt = jnp.stack(cp_l)
    conv_sample = jnp.stack(cs_l)
    gmlp_v_sample = jnp.stack(gv_l)
    return (yp, ys, mem_k_prompt, mem_v_prompt, conv_prompt, conv_sample, gmlp_v_sample)
```

```python
import functools

import jax
import jax.numpy as jnp
from jax import lax
from jax.experimental import pallas as pl
from jax.experimental.pallas import tpu as pltpu

EPS = 1e-6
D_MODEL = 4096
C_CONV = 1536
C_GMLP = 1536
C_X = 1024
CONV_W = 31
G_HEADS = 4
G_HEAD_DIM = C_GMLP // G_HEADS
X_HEADS = 4
X_HEAD_DIM = C_X // X_HEADS
N_MEM = 256
GMLP_CHUNK = 128
N_IN = 3 * C_CONV + 3 * C_GMLP + 2 * C_X

OFF_A, OFF_BG, OFF_GC = 0, C_CONV, 2 * C_CONV
OFF_U, OFF_V, OFF_GG = 3 * C_CONV, 3 * C_CONV + C_GMLP, 3 * C_CONV + 2 * C_GMLP
OFF_Q = 3 * C_CONV + 3 * C_GMLP
OFF_GX = OFF_Q + C_X

LANES = 128
SUBLANES = 8
CONV_TILES = C_CONV // LANES
HIST_ROWS = 32
HIST_PAD = HIST_ROWS - (CONV_W - 1)
VMEM_LIMIT_BYTES = 56 * 1024 * 1024

BF16 = jnp.bfloat16
F32 = jnp.float32


def _rms_scale(x):
    return lax.rsqrt(jnp.mean(x * x, axis=-1, keepdims=True) + EPS)


def _layer_norm(x, g, b):
    xc = x - jnp.mean(x, axis=-1, keepdims=True)
    return xc * lax.rsqrt(jnp.mean(xc * xc, axis=-1, keepdims=True) + EPS) * g + b


def _silu(x):
    return x * jax.nn.sigmoid(x)


def _inproj_kernel(x_ref, g_ref, w_ref, o_ref, h_ref, *, row_chunk):
    @pl.when(pl.program_id(1) == 0)
    def _():
        def body(r, carry):
            rows = pl.ds(pl.multiple_of(r * row_chunk, row_chunk), row_chunk)
            x = x_ref[rows, :]
            h_ref[rows, :] = (x * _rms_scale(x) * g_ref[...]).astype(BF16)
            return carry

        lax.fori_loop(0, x_ref.shape[0] // row_chunk, body, 0)

    o_ref[...] = jnp.dot(h_ref[...], w_ref[...], preferred_element_type=F32)


def _inproj(x2d, g, w_bf16, *, tm, tn):
    m, d = x2d.shape
    n = w_bf16.shape[1]
    return pl.pallas_call(
        functools.partial(_inproj_kernel, row_chunk=64),
        grid=(m // tm, n // tn),
        in_specs=[
            pl.BlockSpec((tm, d), lambda i, j: (i, 0)),
            pl.BlockSpec((1, d), lambda i, j: (0, 0)),
            pl.BlockSpec((d, tn), lambda i, j: (0, j)),
        ],
        out_specs=pl.BlockSpec((tm, tn), lambda i, j: (i, j)),
        out_shape=jax.ShapeDtypeStruct((m, n), F32),
        scratch_shapes=[pltpu.VMEM((tm, d), BF16)],
        compiler_params=pltpu.CompilerParams(
            dimension_semantics=("parallel", "arbitrary"), vmem_limit_bytes=VMEM_LIMIT_BYTES),
        name="inproj",
    )(x2d, g, w_bf16)


def _memkv_kernel(mem_ref, g_ref, wk_ref, wv_ref, k_ref, v_ref, m_ref, *, row_chunk):
    @pl.when(pl.program_id(0) == 0)
    def _():
        def body(r, carry):
            rows = pl.ds(pl.multiple_of(r * row_chunk, row_chunk), row_chunk)
            x = mem_ref[rows, :]
            m_ref[rows, :] = (x * _rms_scale(x) * g_ref[...]).astype(BF16)
            return carry

        lax.fori_loop(0, mem_ref.shape[0] // row_chunk, body, 0)

    k_ref[...] = jnp.dot(m_ref[...], wk_ref[...].astype(BF16), preferred_element_type=F32)
    v_ref[...] = jnp.dot(m_ref[...], wv_ref[...].astype(BF16), preferred_element_type=F32)


def _memkv(mem2d, g, w_mk, w_mv, *, tn):
    m, d = mem2d.shape
    n = w_mk.shape[1]
    return pl.pallas_call(
        functools.partial(_memkv_kernel, row_chunk=64),
        grid=(n // tn,),
        in_specs=[
            pl.BlockSpec((m, d), lambda j: (0, 0)),
            pl.BlockSpec((1, d), lambda j: (0, 0)),
            pl.BlockSpec((d, tn), lambda j: (0, j)),
            pl.BlockSpec((d, tn), lambda j: (0, j)),
        ],
        out_specs=[pl.BlockSpec((m, tn), lambda j: (0, j)), pl.BlockSpec((m, tn), lambda j: (0, j))],
        out_shape=[jax.ShapeDtypeStruct((m, n), F32), jax.ShapeDtypeStruct((m, n), F32)],
        scratch_shapes=[pltpu.VMEM((m, d), BF16)],
        compiler_params=pltpu.CompilerParams(
            dimension_semantics=("arbitrary",), vmem_limit_bytes=VMEM_LIMIT_BYTES),
        name="memkv",
    )(mem2d, g, w_mk, w_mv)


def _mixer_kernel(*refs, tt, nt, from_cache, emit_vn):
    it = iter(refs)
    p_ref = next(it)
    memk_ref = next(it)
    memv_ref = next(it)
    hist_ref = next(it) if from_cache else None
    cw_ref = next(it)
    cb_ref = next(it)
    lcg_ref = next(it)
    lcb_ref = next(it)
    lvg_ref = next(it)
    lvb_ref = next(it)
    ws_ref = next(it)
    bs_ref = next(it)
    gb_ref = next(it)
    mixed_ref = next(it)
    conv_ref = next(it)
    vn_ref = next(it) if emit_vn else None
    s_ref = next(it)
    c_ref = next(it)

    t = pl.program_id(1)

    @pl.when(t == 0)
    def _():
        for c in range(CONV_TILES):
            if from_cache:
                s_ref[c, 0:HIST_ROWS, :] = hist_ref[0, :, c * LANES:(c + 1) * LANES]
            else:
                s_ref[c, 0:HIST_ROWS, :] = jnp.zeros((HIST_ROWS, LANES), F32)

    if nt > 1:
        @pl.when(t > 0)
        def _():
            for c in range(CONV_TILES):
                s_ref[c, 0:HIST_ROWS, :] = s_ref[c, tt:tt + HIST_ROWS, :]

    glu = p_ref[:, OFF_A:OFF_A + C_CONV] * jax.nn.sigmoid(p_ref[:, OFF_BG:OFF_BG + C_CONV])
    for c in range(CONV_TILES):
        s_ref[c, HIST_ROWS:HIST_ROWS + tt, :] = glu[:, c * LANES:(c + 1) * LANES]

    def conv_tile(c, carry):
        w = cw_ref[c]
        taps = [jnp.broadcast_to(w[k:k + 1, :], (SUBLANES, LANES)) for k in range(CONV_W)]
        bias = jnp.broadcast_to(cb_ref[c], (SUBLANES, LANES))
        for r in range(tt // SUBLANES):
            acc = bias
            for k in range(CONV_W):
                acc = acc + taps[k] * s_ref[c, pl.ds(r * SUBLANES + k + HIST_PAD, SUBLANES), :]
            c_ref[c, r * SUBLANES:(r + 1) * SUBLANES, :] = acc
        return carry

    lax.fori_loop(0, CONV_TILES, conv_tile, 0)

    @pl.when(t == nt - 1)
    def _():
        for c in range(CONV_TILES):
            conv_ref[0, :, c * LANES:(c + 1) * LANES] = s_ref[c, tt:tt + HIST_ROWS, :]

    conv = jnp.concatenate([c_ref[c] for c in range(CONV_TILES)], axis=-1)
    yc = _silu(_layer_norm(conv, lcg_ref[...], lcb_ref[...])) * _silu(p_ref[:, OFF_GC:OFF_GC + C_CONV])
    mixed_ref[:, 0:C_CONV] = (yc * _rms_scale(yc) * gb_ref[:, 0:C_CONV]).astype(BF16)

    vn = _layer_norm(p_ref[:, OFF_V:OFF_V + C_GMLP], lvg_ref[...], lvb_ref[...])
    if emit_vn:
        vn_ref[0] = vn
    vn_b = vn.astype(BF16)
    row = lax.broadcasted_iota(jnp.int32, (GMLP_CHUNK, GMLP_CHUNK), 0)
    col = lax.broadcasted_iota(jnp.int32, (GMLP_CHUNK, GMLP_CHUNK), 1)
    yg_heads = []
    for h in range(G_HEADS):
        lanes = slice(h * G_HEAD_DIM, (h + 1) * G_HEAD_DIM)
        wm = jnp.where(row >= col, ws_ref[h], 0.0).astype(BF16)[:tt, :tt]
        z = jnp.dot(wm, vn_b[:, lanes], preferred_element_type=F32) + bs_ref[0:tt, h:h + 1]
        u = p_ref[:, OFF_U + h * G_HEAD_DIM:OFF_U + (h + 1) * G_HEAD_DIM]
        gg = p_ref[:, OFF_GG + h * G_HEAD_DIM:OFF_GG + (h + 1) * G_HEAD_DIM]
        yg_heads.append(u * z * _silu(gg))
    yg = jnp.concatenate(yg_heads, axis=-1)
    mixed_ref[:, C_CONV:C_CONV + C_GMLP] = (
        yg * _rms_scale(yg) * gb_ref[:, C_CONV:C_CONV + C_GMLP]).astype(BF16)

    yx_heads = []
    for h in range(X_HEADS):
        lanes = slice(h * X_HEAD_DIM, (h + 1) * X_HEAD_DIM)
        q = (p_ref[:, OFF_Q + h * X_HEAD_DIM:OFF_Q + (h + 1) * X_HEAD_DIM] * (X_HEAD_DIM ** -0.5)).astype(BF16)
        k = memk_ref[0, :, lanes].astype(BF16)
        v = memv_ref[0, :, lanes].astype(BF16)
        s = lax.dot_general(q, k, (((1,), (1,)), ((), ())), preferred_element_type=F32)
        e = jnp.exp(s - jnp.max(s, axis=-1, keepdims=True))
        prob = e / jnp.sum(e, axis=-1, keepdims=True)
        o = jnp.dot(prob.astype(BF16), v, preferred_element_type=F32)
        yx_heads.append(o * _silu(p_ref[:, OFF_GX + h * X_HEAD_DIM:OFF_GX + (h + 1) * X_HEAD_DIM]))
    yx = jnp.concatenate(yx_heads, axis=-1)
    mixed_ref[:, C_CONV + C_GMLP:D_MODEL] = (
        yx * _rms_scale(yx) * gb_ref[:, C_CONV + C_GMLP:D_MODEL]).astype(BF16)


def _mixer(p2d, memk, memv, hist, weights, *, batch, seq, tt, emit_vn):
    nt = seq // tt
    from_cache = hist is not None
    small = lambda a: pl.BlockSpec(a.shape, lambda b, t, _n=a.ndim: (0,) * _n)
    in_specs = [
        pl.BlockSpec((tt, N_IN), lambda b, t: (b * nt + t, 0)),
        pl.BlockSpec((1, N_MEM, C_X), lambda b, t: (b, 0, 0)),
        pl.BlockSpec((1, N_MEM, C_X), lambda b, t: (b, 0, 0)),
    ]
    args = [p2d, memk, memv]
    if from_cache:
        in_specs.append(pl.BlockSpec((1, HIST_ROWS, C_CONV), lambda b, t: (b, 0, 0)))
        args.append(hist)
    in_specs += [small(w) for w in weights]
    args += list(weights)
    out_specs = [
        pl.BlockSpec((tt, D_MODEL), lambda b, t: (b * nt + t, 0)),
        pl.BlockSpec((1, HIST_ROWS, C_CONV), lambda b, t: (b, 0, 0)),
    ]
    out_shape = [
        jax.ShapeDtypeStruct((batch * seq, D_MODEL), BF16),
        jax.ShapeDtypeStruct((batch, HIST_ROWS, C_CONV), F32),
    ]
    if emit_vn:
        out_specs.append(pl.BlockSpec((1, tt, C_GMLP), lambda b, t: (b, t, 0)))
        out_shape.append(jax.ShapeDtypeStruct((batch, seq, C_GMLP), F32))
    return pl.pallas_call(
        functools.partial(_mixer_kernel, tt=tt, nt=nt, from_cache=from_cache, emit_vn=emit_vn),
        grid=(batch, nt),
        in_specs=in_specs,
        out_specs=out_specs,
        out_shape=out_shape,
        scratch_shapes=[
            pltpu.VMEM((CONV_TILES, HIST_ROWS + tt, LANES), F32),
            pltpu.VMEM((CONV_TILES, tt, LANES), F32),
        ],
        compiler_params=pltpu.CompilerParams(
            dimension_semantics=("arbitrary", "arbitrary"), vmem_limit_bytes=VMEM_LIMIT_BYTES),
        name="mixer_cache" if from_cache else "mixer",
    )(*args)


def _outproj_kernel(m_ref, w_ref, x_ref, g_ref, y_ref, acc_ref, *, nj, tn):
    j = pl.program_id(1)
    acc_ref[j] = jnp.dot(m_ref[...], w_ref[...], preferred_element_type=F32)

    @pl.when(j == nj - 1)
    def _():
        ss = jnp.sum(acc_ref[0] * acc_ref[0], axis=-1, keepdims=True)
        for jj in range(1, nj):
            ss = ss + jnp.sum(acc_ref[jj] * acc_ref[jj], axis=-1, keepdims=True)
        scale = lax.rsqrt(ss / D_MODEL + EPS)
        for jj in range(nj):
            cols = slice(jj * tn, (jj + 1) * tn)
            y_ref[:, cols] = x_ref[:, cols] + acc_ref[jj] * scale * g_ref[:, cols]


def _outproj(mixed, w_bf16, x2d, g, *, tm, tn):
    m, d = x2d.shape
    nj = d // tn
    return pl.pallas_call(
        functools.partial(_outproj_kernel, nj=nj, tn=tn),
        grid=(m // tm, nj),
        in_specs=[
            pl.BlockSpec((tm, d), lambda i, j: (i, 0)),
            pl.BlockSpec((d, tn), lambda i, j: (0, j)),
            pl.BlockSpec((tm, d), lambda i, j: (i, 0)),
            pl.BlockSpec((1, d), lambda i, j: (0, 0)),
        ],
        out_specs=pl.BlockSpec((tm, d), lambda i, j: (i, 0)),
        out_shape=jax.ShapeDtypeStruct((m, d), F32),
        scratch_shapes=[pltpu.VMEM((nj, tm, tn), F32)],
        compiler_params=pltpu.CompilerParams(
            dimension_semantics=("parallel", "arbitrary"), vmem_limit_bytes=VMEM_LIMIT_BYTES),
        name="outproj",
    )(mixed, w_bf16, x2d, g)


def _layer(x, memk, memv, hist, lw, *, tt, emit_vn):
    batch, seq, d = x.shape
    x2d = x.reshape(batch * seq, d)
    p = _inproj(x2d, lw["g_pre"], lw["w_in"], tm=512, tn=1024)
    outs = _mixer(p, memk, memv, hist, lw["mixer"], batch=batch, seq=seq, tt=tt, emit_vn=emit_vn)
    y = _outproj(outs[0], lw["w_out"], x2d, lw["g_post"], tm=256, tn=1024)
    return (y.reshape(batch, seq, d),) + tuple(outs[1:])


def kernel(x_prompt, x_sample, mem_prompt, cache_mem_k, cache_mem_v, cache_conv, g_pre, w_in, conv_w, conv_b,
           ln_conv_g, ln_conv_b, ln_v_g, ln_v_b, w_spatial, b_spatial, g_mem, w_mk, w_mv, g_branch, w_out, g_post):
    depth = w_in.shape[0]
    yp, ys = x_prompt, x_sample
    batch = x_prompt.shape[0]
    dec_batch = x_sample.shape[0]
    mk_l, mv_l, cp_l, cs_l, gv_l = [], [], [], [], []
    for l in range(depth):
        row = lambda a: a[l].reshape(1, -1)
        lw = {
            "g_pre": row(g_pre),
            "w_in": w_in[l].astype(BF16),
            "w_out": w_out[l].astype(BF16),
            "g_post": row(g_post),
            "mixer": (
                conv_w[l].reshape(CONV_W, CONV_TILES, LANES).transpose(1, 0, 2),
                conv_b[l].reshape(CONV_TILES, 1, LANES),
                row(ln_conv_g), row(ln_conv_b), row(ln_v_g), row(ln_v_b),
                w_spatial[l], b_spatial[l].T, row(g_branch),
            ),
        }
        mk, mv = _memkv(mem_prompt.reshape(batch * N_MEM, D_MODEL), row(g_mem), w_mk[l], w_mv[l], tn=256)
        yp, cp = _layer(yp, mk.reshape(batch, N_MEM, C_X), mv.reshape(batch, N_MEM, C_X), None, lw,
                        tt=GMLP_CHUNK, emit_vn=False)
        hist = jnp.pad(cache_conv[l], ((0, 0), (HIST_PAD, 0), (0, 0)))
        ys, cs, gv = _layer(ys, cache_mem_k[l].reshape(dec_batch, N_MEM, C_X),
                            cache_mem_v[l].reshape(dec_batch, N_MEM, C_X), hist, lw,
                            tt=x_sample.shape[1], emit_vn=True)
        mk_l.append(mk.reshape(batch, N_MEM, X_HEADS, X_HEAD_DIM))
        mv_l.append(mv.reshape(batch, N_MEM, X_HEADS, X_HEAD_DIM))
        cp_l.append(cp[:, HIST_PAD:, :])
        cs_l.append(cs[:, HIST_PAD:, :])
        gv_l.append(gv)
    return (yp, ys, jnp.stack(mk_l), jnp.stack(mv_l), jnp.stack(cp_l), jnp.stack(cs_l), jnp.stack(gv_l))
```

```python
import functools

import jax
import jax.numpy as jnp
from jax import lax
from jax.experimental import pallas as pl
from jax.experimental.pallas import tpu as pltpu

EPS = 1e-6
D_MODEL = 4096
C_CONV = 1536
C_GMLP = 1536
C_X = 1024
CONV_W = 31
G_HEADS = 4
G_HEAD_DIM = C_GMLP // G_HEADS
X_HEADS = 4
X_HEAD_DIM = C_X // X_HEADS
N_MEM = 256
GMLP_CHUNK = 128
N_IN = 3 * C_CONV + 3 * C_GMLP + 2 * C_X

OFF_A, OFF_BG, OFF_GC = 0, C_CONV, 2 * C_CONV
OFF_U, OFF_V, OFF_GG = 3 * C_CONV, 3 * C_CONV + C_GMLP, 3 * C_CONV + 2 * C_GMLP
OFF_Q = 3 * C_CONV + 3 * C_GMLP
OFF_GX = OFF_Q + C_X

LANES = 128
SUBLANES = 8
CONV_TILES = C_CONV // LANES
HIST_ROWS = 32
HIST_PAD = HIST_ROWS - (CONV_W - 1)
V7X_VMEM_BYTES = 64 * 1024 * 1024
VMEM_LIMIT_BYTES = V7X_VMEM_BYTES - 8 * 1024 * 1024
ROW_CHUNK = 64

BF16 = jnp.bfloat16
F32 = jnp.float32


def _rms_scale(x):
    return lax.rsqrt(jnp.mean(x * x, axis=-1, keepdims=True) + EPS)


def _layer_norm(x, g, b):
    xc = x - jnp.mean(x, axis=-1, keepdims=True)
    return xc * lax.rsqrt(jnp.mean(xc * xc, axis=-1, keepdims=True) + EPS) * g + b


def _silu(x):
    return x * jax.nn.sigmoid(x)


def _row_chunks(n_rows, body):
    def step(r, carry):
        body(pl.ds(pl.multiple_of(r * ROW_CHUNK, ROW_CHUNK), ROW_CHUNK))
        return carry

    lax.fori_loop(0, n_rows // ROW_CHUNK, step, 0)


def _rms_rows_to_bf16(x_ref, g_ref, h_ref):
    def body(rows):
        x = x_ref[rows, :]
        h_ref[rows, :] = (x * _rms_scale(x) * g_ref[...]).astype(BF16)

    _row_chunks(x_ref.shape[0], body)


def _inproj_kernel(x_ref, g_ref, w_ref, o_ref, *rest, emit_w_bf16):
    h_ref = rest[-1]

    @pl.when(pl.program_id(1) == 0)
    def _():
        _rms_rows_to_bf16(x_ref, g_ref, h_ref)

    w = w_ref[...].astype(BF16)
    if emit_w_bf16:
        rest[0][...] = w
    o_ref[...] = jnp.dot(h_ref[...], w, preferred_element_type=F32).astype(o_ref.dtype)


def _inproj(x2d, g, w, *, tm, tn, emit_w_bf16=False):
    m, d = x2d.shape
    n = w.shape[1]
    out_specs = [pl.BlockSpec((tm, tn), lambda i, j: (i, j))]
    out_shape = [jax.ShapeDtypeStruct((m, n), BF16)]
    x_mode = {}
    if emit_w_bf16:
        assert m == tm, "the bf16 weight copy is written once per column block, by the only row tile"
        out_specs.append(pl.BlockSpec((d, tn), lambda i, j: (0, j)))
        out_shape.append(jax.ShapeDtypeStruct((d, n), BF16))
        x_mode = dict(pipeline_mode=pl.Buffered(1))
    return pl.pallas_call(
        functools.partial(_inproj_kernel, emit_w_bf16=emit_w_bf16),
        grid=(m // tm, n // tn),
        in_specs=[
            pl.BlockSpec((tm, d), lambda i, j: (i, 0), **x_mode),
            pl.BlockSpec((1, d), lambda i, j: (0, 0)),
            pl.BlockSpec((d, tn), lambda i, j: (0, j)),
        ],
        out_specs=out_specs,
        out_shape=out_shape,
        scratch_shapes=[pltpu.VMEM((tm, d), BF16)],
        compiler_params=pltpu.CompilerParams(
            dimension_semantics=("parallel", "arbitrary"), vmem_limit_bytes=VMEM_LIMIT_BYTES),
        name="inproj_wcast" if emit_w_bf16 else "inproj",
    )(x2d, g, w)


def _memkv_kernel(mem_ref, g_ref, wk_ref, wv_ref, k_ref, v_ref, kb_ref, vb_ref, m_ref):
    @pl.when(pl.program_id(0) == 0)
    def _():
        _rms_rows_to_bf16(mem_ref, g_ref, m_ref)

    k = jnp.dot(m_ref[...], wk_ref[...].astype(BF16), preferred_element_type=F32)
    v = jnp.dot(m_ref[...], wv_ref[...].astype(BF16), preferred_element_type=F32)
    k_ref[...] = k
    v_ref[...] = v
    kb_ref[...] = k.astype(BF16)
    vb_ref[...] = v.astype(BF16)


def _memkv(mem2d, g, w_mk, w_mv, *, tn):
    m, d = mem2d.shape
    n = w_mk.shape[1]
    col = lambda j: (0, j)
    return pl.pallas_call(
        _memkv_kernel,
        grid=(n // tn,),
        in_specs=[
            pl.BlockSpec((m, d), lambda j: (0, 0)),
            pl.BlockSpec((1, d), lambda j: (0, 0)),
            pl.BlockSpec((d, tn), col),
            pl.BlockSpec((d, tn), col),
        ],
        out_specs=[pl.BlockSpec((m, tn), col)] * 4,
        out_shape=[jax.ShapeDtypeStruct((m, n), F32)] * 2 + [jax.ShapeDtypeStruct((m, n), BF16)] * 2,
        scratch_shapes=[pltpu.VMEM((m, d), BF16)],
        compiler_params=pltpu.CompilerParams(
            dimension_semantics=("arbitrary",), vmem_limit_bytes=VMEM_LIMIT_BYTES),
        name="memkv",
    )(mem2d, g, w_mk, w_mv)


_MIXER_INPUTS = ("p", "memk", "memv", "hist", "cw", "cb", "lcg", "lcb", "lvg", "lvb", "ws", "bs", "gb", "wout_f32")


def _mixer_kernel(*refs, tt, nt, from_cache, emit_vn, cast_wout):
    absent = set() if from_cache else {"hist"}
    absent |= set() if cast_wout else {"wout_f32"}
    names = [n for n in _MIXER_INPUTS if n not in absent]
    names += ["mixed", "conv"] + (["vn"] if emit_vn else []) + (["wout_b"] if cast_wout else []) + ["s", "c"]
    r = dict(zip(names, refs))
    if cast_wout:
        r["wout_b"][...] = r["wout_f32"][...].astype(BF16)
    p_ref = r["p"]
    mixed_ref = r["mixed"]
    gb_ref = r["gb"]
    s_ref = r["s"]
    c_ref = r["c"]
    pf = lambda off, width: p_ref[:, off:off + width].astype(F32)

    t = pl.program_id(1)

    @pl.when(t == 0)
    def _():
        for c in range(CONV_TILES):
            if from_cache:
                s_ref[c, 0:HIST_ROWS, :] = r["hist"][0, :, c * LANES:(c + 1) * LANES]
            else:
                s_ref[c, 0:HIST_ROWS, :] = jnp.zeros((HIST_ROWS, LANES), F32)

    if nt > 1:
        @pl.when(t > 0)
        def _():
            for c in range(CONV_TILES):
                s_ref[c, 0:HIST_ROWS, :] = s_ref[c, tt:tt + HIST_ROWS, :]

    glu = pf(OFF_A, C_CONV) * jax.nn.sigmoid(pf(OFF_BG, C_CONV))
    for c in range(CONV_TILES):
        s_ref[c, HIST_ROWS:HIST_ROWS + tt, :] = glu[:, c * LANES:(c + 1) * LANES]

    def conv_tile(c, carry):
        w = r["cw"][c]
        taps = [jnp.broadcast_to(w[k:k + 1, :], (SUBLANES, LANES)) for k in range(CONV_W)]
        bias = jnp.broadcast_to(r["cb"][c], (SUBLANES, LANES))
        for g in range(tt // SUBLANES):
            acc = bias
            for k in range(CONV_W):
                acc = acc + taps[k] * s_ref[c, pl.ds(g * SUBLANES + k + HIST_PAD, SUBLANES), :]
            c_ref[c, g * SUBLANES:(g + 1) * SUBLANES, :] = acc
        return carry

    lax.fori_loop(0, CONV_TILES, conv_tile, 0)

    @pl.when(t == nt - 1)
    def _():
        for c in range(CONV_TILES):
            r["conv"][0, :, c * LANES:(c + 1) * LANES] = s_ref[c, tt:tt + HIST_ROWS, :]

    conv = jnp.concatenate([c_ref[c] for c in range(CONV_TILES)], axis=-1)
    yc = _silu(_layer_norm(conv, r["lcg"][...], r["lcb"][...])) * _silu(pf(OFF_GC, C_CONV))
    mixed_ref[:, 0:C_CONV] = (yc * _rms_scale(yc) * gb_ref[:, 0:C_CONV]).astype(BF16)

    vn = _layer_norm(pf(OFF_V, C_GMLP), r["lvg"][...], r["lvb"][...])
    if emit_vn:
        r["vn"][0] = vn
    vn_b = vn.astype(BF16)
    row = lax.broadcasted_iota(jnp.int32, (GMLP_CHUNK, GMLP_CHUNK), 0)
    col = lax.broadcasted_iota(jnp.int32, (GMLP_CHUNK, GMLP_CHUNK), 1)
    yg_heads = []
    for h in range(G_HEADS):
        lanes = slice(h * G_HEAD_DIM, (h + 1) * G_HEAD_DIM)
        wm = jnp.where(row >= col, r["ws"][h], 0.0).astype(BF16)[:tt, :tt]
        z = jnp.dot(wm, vn_b[:, lanes], preferred_element_type=F32) + r["bs"][0:tt, h:h + 1]
        yg_heads.append(pf(OFF_U + h * G_HEAD_DIM, G_HEAD_DIM) * z * _silu(pf(OFF_GG + h * G_HEAD_DIM, G_HEAD_DIM)))
    yg = jnp.concatenate(yg_heads, axis=-1)
    mixed_ref[:, C_CONV:C_CONV + C_GMLP] = (yg * _rms_scale(yg) * gb_ref[:, C_CONV:C_CONV + C_GMLP]).astype(BF16)

    yx_heads = []
    for h in range(X_HEADS):
        lanes = slice(h * X_HEAD_DIM, (h + 1) * X_HEAD_DIM)
        q = (pf(OFF_Q + h * X_HEAD_DIM, X_HEAD_DIM) * (X_HEAD_DIM ** -0.5)).astype(BF16)
        k = r["memk"][0, :, lanes].astype(BF16)
        v = r["memv"][0, :, lanes].astype(BF16)
        s = lax.dot_general(q, k, (((1,), (1,)), ((), ())), preferred_element_type=F32)
        e = jnp.exp(s - jnp.max(s, axis=-1, keepdims=True))
        prob = e / jnp.sum(e, axis=-1, keepdims=True)
        o = jnp.dot(prob.astype(BF16), v, preferred_element_type=F32)
        yx_heads.append(o * _silu(pf(OFF_GX + h * X_HEAD_DIM, X_HEAD_DIM)))
    yx = jnp.concatenate(yx_heads, axis=-1)
    mixed_ref[:, C_CONV + C_GMLP:D_MODEL] = (yx * _rms_scale(yx) * gb_ref[:, C_CONV + C_GMLP:D_MODEL]).astype(BF16)


def _mixer(p2d, memk, memv, mem_base, hist, weights, wout_f32, *, batch, seq, tt, emit_vn):
    nt = seq // tt
    from_cache = hist is not None
    cast_wout = wout_f32 is not None
    const = lambda a: pl.BlockSpec(a.shape, lambda b, t, _n=a.ndim: (0,) * _n)
    in_specs = [
        pl.BlockSpec((tt, N_IN), lambda b, t: (b * nt + t, 0)),
        pl.BlockSpec((1, N_MEM, C_X), lambda b, t: (mem_base + b, 0, 0)),
        pl.BlockSpec((1, N_MEM, C_X), lambda b, t: (mem_base + b, 0, 0)),
    ]
    args = [p2d, memk, memv]
    if from_cache:
        in_specs.append(pl.BlockSpec((1, HIST_ROWS, C_CONV), lambda b, t: (b, 0, 0)))
        args.append(hist)
    in_specs += [const(w) for w in weights]
    args += list(weights)
    if cast_wout:
        w_rows = wout_f32.shape[0] // (batch * nt)
        assert w_rows * batch * nt == wout_f32.shape[0] and w_rows % 16 == 0
        w_block = pl.BlockSpec((w_rows, wout_f32.shape[1]), lambda b, t: (b * nt + t, 0))
        in_specs.append(w_block)
        args.append(wout_f32)
    out_specs = [
        pl.BlockSpec((tt, D_MODEL), lambda b, t: (b * nt + t, 0)),
        pl.BlockSpec((1, HIST_ROWS, C_CONV), lambda b, t: (b, 0, 0)),
    ]
    out_shape = [
        jax.ShapeDtypeStruct((batch * seq, D_MODEL), BF16),
        jax.ShapeDtypeStruct((batch, HIST_ROWS, C_CONV), F32),
    ]
    if emit_vn:
        out_specs.append(pl.BlockSpec((1, tt, C_GMLP), lambda b, t: (b, t, 0)))
        out_shape.append(jax.ShapeDtypeStruct((batch, seq, C_GMLP), F32))
    if cast_wout:
        out_specs.append(w_block)
        out_shape.append(jax.ShapeDtypeStruct(wout_f32.shape, BF16))
    return pl.pallas_call(
        functools.partial(_mixer_kernel, tt=tt, nt=nt, from_cache=from_cache, emit_vn=emit_vn, cast_wout=cast_wout),
        grid=(batch, nt),
        in_specs=in_specs,
        out_specs=out_specs,
        out_shape=out_shape,
        scratch_shapes=[
            pltpu.VMEM((CONV_TILES, HIST_ROWS + tt, LANES), F32),
            pltpu.VMEM((CONV_TILES, tt, LANES), F32),
        ],
        compiler_params=pltpu.CompilerParams(
            dimension_semantics=("arbitrary", "arbitrary"), vmem_limit_bytes=VMEM_LIMIT_BYTES),
        name="mixer_cache" if from_cache else "mixer",
    )(*args)


def _outproj_kernel(m_ref, w_ref, x_ref, g_ref, y_ref, *, nj, tn):
    j = pl.program_id(1)
    y_ref[:, pl.ds(pl.multiple_of(j * tn, tn), tn)] = jnp.dot(m_ref[...], w_ref[...], preferred_element_type=F32)

    @pl.when(j == nj - 1)
    def _():
        def body(rows):
            out = y_ref[rows, :]
            y_ref[rows, :] = x_ref[rows, :] + out * _rms_scale(out) * g_ref[...]

        _row_chunks(y_ref.shape[0], body)


def _outproj(mixed, w_bf16, x2d, g, *, tm, tn):
    m, d = x2d.shape
    nj = d // tn
    return pl.pallas_call(
        functools.partial(_outproj_kernel, nj=nj, tn=tn),
        grid=(m // tm, nj),
        in_specs=[
            pl.BlockSpec((tm, d), lambda i, j: (i, 0)),
            pl.BlockSpec((d, tn), lambda i, j: (0, j)),
            pl.BlockSpec((tm, d), lambda i, j: (i, 0)),
            pl.BlockSpec((1, d), lambda i, j: (0, 0)),
        ],
        out_specs=pl.BlockSpec((tm, d), lambda i, j: (i, 0)),
        out_shape=jax.ShapeDtypeStruct((m, d), F32),
        compiler_params=pltpu.CompilerParams(
            dimension_semantics=("parallel", "arbitrary"), vmem_limit_bytes=VMEM_LIMIT_BYTES),
        name="outproj",
    )(mixed, w_bf16, x2d, g)


def _layer(x, memk, memv, mem_base, hist, lw, *, tt, emit_vn, cast_weights):
    batch, seq, d = x.shape
    x2d = x.reshape(batch * seq, d)
    if cast_weights:
        p, w_in_b = _inproj(x2d, lw["g_pre"], lw["w_in"], tm=batch * seq, tn=512, emit_w_bf16=True)
    else:
        (p,), w_in_b = _inproj(x2d, lw["g_pre"], lw["w_in"], tm=1024, tn=512), lw["w_in"]
    outs = list(_mixer(p, memk, memv, mem_base, hist, lw["mixer"], lw["w_out"] if cast_weights else None,
                       batch=batch, seq=seq, tt=tt, emit_vn=emit_vn))
    w_out_b = outs.pop() if cast_weights else lw["w_out"]
    y = _outproj(outs[0], w_out_b, x2d, lw["g_post"], tm=512, tn=512)
    return (y.reshape(batch, seq, d),) + tuple(outs[1:]) + (w_in_b, w_out_b)


def kernel(x_prompt, x_sample, mem_prompt, cache_mem_k, cache_mem_v, cache_conv, g_pre, w_in, conv_w, conv_b,
           ln_conv_g, ln_conv_b, ln_v_g, ln_v_b, w_spatial, b_spatial, g_mem, w_mk, w_mv, g_branch, w_out, g_post):
    depth = w_in.shape[0]
    yp, ys = x_prompt, x_sample
    batch = x_prompt.shape[0]
    dec_batch = x_sample.shape[0]
    cache_k = cache_mem_k.reshape(depth * dec_batch, N_MEM, C_X)
    cache_v = cache_mem_v.reshape(depth * dec_batch, N_MEM, C_X)
    mk_l, mv_l, cp_l, cs_l, gv_l = [], [], [], [], []
    for l in range(depth):
        row = lambda a: a[l].reshape(1, -1)
        lw = {
            "g_pre": row(g_pre),
            "w_in": w_in[l],
            "w_out": w_out[l],
            "g_post": row(g_post),
            "mixer": (
                conv_w[l].reshape(CONV_W, CONV_TILES, LANES).transpose(1, 0, 2),
                conv_b[l].reshape(CONV_TILES, 1, LANES),
                row(ln_conv_g), row(ln_conv_b), row(ln_v_g), row(ln_v_b),
                w_spatial[l], b_spatial[l].T, row(g_branch),
            ),
        }
        hist = jnp.pad(cache_conv[l], ((0, 0), (HIST_PAD, 0), (0, 0)))
        ys, cs, gv, w_in_b, w_out_b = _layer(ys, cache_k, cache_v, l * dec_batch, hist, lw,
                                             tt=x_sample.shape[1], emit_vn=True, cast_weights=True)
        mk, mv, mk_b, mv_b = _memkv(mem_prompt.reshape(batch * N_MEM, D_MODEL), row(g_mem), w_mk[l], w_mv[l], tn=256)
        yp, cp, _, _ = _layer(yp, mk_b.reshape(batch, N_MEM, C_X), mv_b.reshape(batch, N_MEM, C_X), 0, None,
                              dict(lw, w_in=w_in_b, w_out=w_out_b), tt=GMLP_CHUNK, emit_vn=False, cast_weights=False)
        mk_l.append(mk.reshape(batch, N_MEM, X_HEADS, X_HEAD_DIM))
        mv_l.append(mv.reshape(batch, N_MEM, X_HEADS, X_HEAD_DIM))
        cp_l.append(cp[:, HIST_PAD:, :])
        cs_l.append(cs[:, HIST_PAD:, :])
        gv_l.append(gv)
    return (yp, ys, jnp.stack(mk_l), jnp.stack(mv_l), jnp.stack(cp_l), jnp.stack(cs_l), jnp.stack(gv_l))
```

```python
import functools

import jax
import jax.numpy as jnp
from jax import lax
from jax.experimental import pallas as pl
from jax.experimental.pallas import tpu as pltpu

EPS = 1e-6
D_MODEL = 4096
C_CONV = 1536
C_GMLP = 1536
C_X = 1024
CONV_W = 31
G_HEADS = 4
G_HEAD_DIM = C_GMLP // G_HEADS
X_HEADS = 4
X_HEAD_DIM = C_X // X_HEADS
N_MEM = 256
GMLP_CHUNK = 128
N_IN = 3 * C_CONV + 3 * C_GMLP + 2 * C_X

OFF_A, OFF_BG, OFF_GC = 0, C_CONV, 2 * C_CONV
OFF_U, OFF_V, OFF_GG = 3 * C_CONV, 3 * C_CONV + C_GMLP, 3 * C_CONV + 2 * C_GMLP
OFF_Q = 3 * C_CONV + 3 * C_GMLP
OFF_GX = OFF_Q + C_X

LANES = 128
SUBLANES = 8
CONV_TILES = C_CONV // LANES
HIST_ROWS = 32
HIST_PAD = HIST_ROWS - (CONV_W - 1)
CONV_CHAINS = 4
V7X_VMEM_BYTES = 64 * 1024 * 1024
VMEM_LIMIT_BYTES = V7X_VMEM_BYTES - 8 * 1024 * 1024
ROW_CHUNK = 64

BF16 = jnp.bfloat16
F32 = jnp.float32


def _rms_scale(x):
    return lax.rsqrt(jnp.mean(x * x, axis=-1, keepdims=True) + EPS)


def _layer_norm(x, g, b):
    xc = x - jnp.mean(x, axis=-1, keepdims=True)
    return xc * lax.rsqrt(jnp.mean(xc * xc, axis=-1, keepdims=True) + EPS) * g + b


def _sigmoid(x):
    return 0.5 * jnp.tanh(0.5 * x) + 0.5


def _silu(x):
    return x * _sigmoid(x)


def _exact_zero_of(x):
    bits = lax.bitcast_convert_type(x, jnp.uint32)
    bits = lax.shift_right_logical(lax.shift_right_logical(bits, jnp.uint32(16)), jnp.uint32(16))
    return lax.bitcast_convert_type(bits, F32)


def _row_groups(n_rows, group, body):
    def step(r, carry):
        base = pl.multiple_of(r * ROW_CHUNK, ROW_CHUNK)
        for s in range(ROW_CHUNK // group):
            body(pl.ds(base + s * group, group))
        return carry

    lax.fori_loop(0, n_rows // ROW_CHUNK, step, 0)


def _rms_rows_to_bf16(x_ref, g_ref, h_ref):
    def body(rows):
        x = x_ref[rows, :]
        h_ref[rows, :] = (x * _rms_scale(x) * g_ref[...]).astype(BF16)

    _row_groups(x_ref.shape[0], 2 * SUBLANES, body)


def _inproj_kernel(x_ref, g_ref, w_ref, o_ref, *rest, emit_w_bf16):
    h_ref = rest[-1]

    @pl.when(pl.program_id(1) == 0)
    def _():
        _rms_rows_to_bf16(x_ref, g_ref, h_ref)

    w = w_ref[...].astype(BF16)
    if emit_w_bf16:
        rest[0][...] = w
    o_ref[...] = jnp.dot(h_ref[...], w, preferred_element_type=F32).astype(o_ref.dtype)


def _inproj(x2d, g, w, *, tm, tn, emit_w_bf16=False):
    m, d = x2d.shape
    n = w.shape[1]
    out_specs = [pl.BlockSpec((tm, tn), lambda i, j: (i, j))]
    out_shape = [jax.ShapeDtypeStruct((m, n), BF16)]
    x_mode = {}
    if emit_w_bf16:
        assert m == tm, "the bf16 weight copy is written once per column block, by the only row tile"
        out_specs.append(pl.BlockSpec((d, tn), lambda i, j: (0, j)))
        out_shape.append(jax.ShapeDtypeStruct((d, n), BF16))
        x_mode = dict(pipeline_mode=pl.Buffered(1))
    return pl.pallas_call(
        functools.partial(_inproj_kernel, emit_w_bf16=emit_w_bf16),
        grid=(m // tm, n // tn),
        in_specs=[
            pl.BlockSpec((tm, d), lambda i, j: (i, 0), **x_mode),
            pl.BlockSpec((1, d), lambda i, j: (0, 0)),
            pl.BlockSpec((d, tn), lambda i, j: (0, j)),
        ],
        out_specs=out_specs,
        out_shape=out_shape,
        scratch_shapes=[pltpu.VMEM((tm, d), BF16)],
        compiler_params=pltpu.CompilerParams(
            dimension_semantics=("parallel", "arbitrary"), vmem_limit_bytes=VMEM_LIMIT_BYTES),
        name="inproj_wcast" if emit_w_bf16 else "inproj",
    )(x2d, g, w)


def _memkv_kernel(mem_ref, g_ref, wk_ref, wv_ref, k_ref, v_ref, kb_ref, vb_ref, m_ref):
    @pl.when(pl.program_id(0) == 0)
    def _():
        _rms_rows_to_bf16(mem_ref, g_ref, m_ref)

    k = jnp.dot(m_ref[...], wk_ref[...].astype(BF16), preferred_element_type=F32)
    v = jnp.dot(m_ref[...], wv_ref[...].astype(BF16), preferred_element_type=F32)
    k_ref[...] = k
    v_ref[...] = v
    kb_ref[...] = k.astype(BF16)
    vb_ref[...] = v.astype(BF16)


def _memkv(mem2d, g, w_mk, w_mv, *, tn):
    m, d = mem2d.shape
    n = w_mk.shape[1]
    col = lambda j: (0, j)
    return pl.pallas_call(
        _memkv_kernel,
        grid=(n // tn,),
        in_specs=[
            pl.BlockSpec((m, d), lambda j: (0, 0)),
            pl.BlockSpec((1, d), lambda j: (0, 0)),
            pl.BlockSpec((d, tn), col),
            pl.BlockSpec((d, tn), col),
        ],
        out_specs=[pl.BlockSpec((m, tn), col)] * 4,
        out_shape=[jax.ShapeDtypeStruct((m, n), F32)] * 2 + [jax.ShapeDtypeStruct((m, n), BF16)] * 2,
        scratch_shapes=[pltpu.VMEM((m, d), BF16)],
        compiler_params=pltpu.CompilerParams(
            dimension_semantics=("arbitrary",), vmem_limit_bytes=VMEM_LIMIT_BYTES),
        name="memkv",
    )(mem2d, g, w_mk, w_mv)


def _mem_head(ref, h):
    if len(ref.shape) == 3:
        return ref[0, :, h * X_HEAD_DIM:(h + 1) * X_HEAD_DIM]
    return ref[0, 0, :, h, :]


_MIXER_INPUTS = ("p", "memk", "memv", "hist", "cw", "cb", "lcg", "lcb", "lvg", "lvb", "ws", "bs", "gb", "wout_f32")


def _mixer_kernel(*refs, tt, nt, from_cache, emit_vn, cast_wout):
    absent = set() if from_cache else {"hist"}
    absent |= set() if cast_wout else {"wout_f32"}
    names = [n for n in _MIXER_INPUTS if n not in absent]
    names += ["mixed", "conv"] + (["vn"] if emit_vn else []) + (["wout_b"] if cast_wout else []) + ["s"]
    r = dict(zip(names, refs))
    if cast_wout:
        r["wout_b"][...] = r["wout_f32"][...].astype(BF16)
    p_ref = r["p"]
    mixed_ref = r["mixed"]
    gb_ref = r["gb"]
    s_ref = r["s"]
    pf = lambda off, width: p_ref[:, off:off + width].astype(F32)

    t = pl.program_id(1)

    @pl.when(t == 0)
    def _():
        for c in range(CONV_TILES):
            if from_cache:
                s_ref[c, 0:HIST_ROWS, :] = r["hist"][0, :, c * LANES:(c + 1) * LANES]
            else:
                s_ref[c, 0:HIST_ROWS, :] = jnp.zeros((HIST_ROWS, LANES), F32)

    if nt > 1:
        @pl.when(t > 0)
        def _():
            for c in range(CONV_TILES):
                s_ref[c, 0:HIST_ROWS, :] = s_ref[c, tt:tt + HIST_ROWS, :]

    glu = pf(OFF_A, C_CONV) * _sigmoid(pf(OFF_BG, C_CONV))
    for c in range(CONV_TILES):
        s_ref[c, HIST_ROWS:HIST_ROWS + tt, :] = glu[:, c * LANES:(c + 1) * LANES]

    conv_tiles, chain_tails = [], []
    for c in range(CONV_TILES):
        w = r["cw"][c]
        taps = [jnp.broadcast_to(w[k:k + 1, :], (SUBLANES, LANES)) for k in range(CONV_W)]
        bias = jnp.broadcast_to(r["cb"][c], (SUBLANES, LANES))
        groups = []
        for g in range(tt // SUBLANES):
            acc = bias if len(chain_tails) < CONV_CHAINS else bias + _exact_zero_of(chain_tails[-CONV_CHAINS])
            for k in range(CONV_W):
                lo = g * SUBLANES + k + HIST_PAD
                acc = acc + taps[k] * s_ref[c, lo:lo + SUBLANES, :]
            groups.append(acc)
            chain_tails.append(acc)
        conv_tiles.append(jnp.concatenate(groups, axis=0))
    conv = jnp.concatenate(conv_tiles, axis=-1)

    @pl.when(t == nt - 1)
    def _():
        for c in range(CONV_TILES):
            r["conv"][0, :, c * LANES:(c + 1) * LANES] = s_ref[c, tt:tt + HIST_ROWS, :]

    yc = _silu(_layer_norm(conv, r["lcg"][...], r["lcb"][...])) * _silu(pf(OFF_GC, C_CONV))
    mixed_ref[:, 0:C_CONV] = (yc * _rms_scale(yc) * gb_ref[:, 0:C_CONV]).astype(BF16)

    vn = _layer_norm(pf(OFF_V, C_GMLP), r["lvg"][...], r["lvb"][...])
    if emit_vn:
        r["vn"][0] = vn
    vn_b = vn.astype(BF16)
    row = lax.broadcasted_iota(jnp.int32, (GMLP_CHUNK, GMLP_CHUNK), 0)
    col = lax.broadcasted_iota(jnp.int32, (GMLP_CHUNK, GMLP_CHUNK), 1)
    yg_heads = []
    for h in range(G_HEADS):
        lanes = slice(h * G_HEAD_DIM, (h + 1) * G_HEAD_DIM)
        wm = jnp.where(row >= col, r["ws"][h], 0.0).astype(BF16)[:tt, :tt]
        z = jnp.dot(wm, vn_b[:, lanes], preferred_element_type=F32) + r["bs"][0:tt, h:h + 1]
        yg_heads.append(pf(OFF_U + h * G_HEAD_DIM, G_HEAD_DIM) * z * _silu(pf(OFF_GG + h * G_HEAD_DIM, G_HEAD_DIM)))
    yg = jnp.concatenate(yg_heads, axis=-1)
    mixed_ref[:, C_CONV:C_CONV + C_GMLP] = (yg * _rms_scale(yg) * gb_ref[:, C_CONV:C_CONV + C_GMLP]).astype(BF16)

    scores = []
    for h in range(X_HEADS):
        q = (pf(OFF_Q + h * X_HEAD_DIM, X_HEAD_DIM) * (X_HEAD_DIM ** -0.5)).astype(BF16)
        k = _mem_head(r["memk"], h).astype(BF16)
        scores.append(lax.dot_general(q, k, (((1,), (1,)), ((), ())), preferred_element_type=F32))
    probs = []
    for s in scores:
        e = jnp.exp(s - jnp.max(s, axis=-1, keepdims=True))
        probs.append((e / jnp.sum(e, axis=-1, keepdims=True)).astype(BF16))
    yx_heads = []
    for h in range(X_HEADS):
        o = jnp.dot(probs[h], _mem_head(r["memv"], h).astype(BF16), preferred_element_type=F32)
        yx_heads.append(o * _silu(pf(OFF_GX + h * X_HEAD_DIM, X_HEAD_DIM)))
    yx = jnp.concatenate(yx_heads, axis=-1)
    mixed_ref[:, C_CONV + C_GMLP:D_MODEL] = (yx * _rms_scale(yx) * gb_ref[:, C_CONV + C_GMLP:D_MODEL]).astype(BF16)


def _mixer(p2d, memk, memv, mem_layer, hist, weights, wout_f32, *, batch, seq, tt, emit_vn):
    nt = seq // tt
    from_cache = hist is not None
    cast_wout = wout_f32 is not None
    const = lambda a: pl.BlockSpec(a.shape, lambda b, t, _n=a.ndim: (0,) * _n)
    in_specs = [pl.BlockSpec((tt, N_IN), lambda b, t: (b * nt + t, 0))]
    for mem in (memk, memv):
        if mem.ndim == 3:
            in_specs.append(pl.BlockSpec((1, N_MEM, C_X), lambda b, t: (b, 0, 0)))
        else:
            in_specs.append(pl.BlockSpec((1, 1, N_MEM, X_HEADS, X_HEAD_DIM), lambda b, t: (mem_layer, b, 0, 0, 0)))
    args = [p2d, memk, memv]
    if from_cache:
        in_specs.append(pl.BlockSpec((1, HIST_ROWS, C_CONV), lambda b, t: (b, 0, 0)))
        args.append(hist)
    in_specs += [const(w) for w in weights]
    args += list(weights)
    if cast_wout:
        w_rows = wout_f32.shape[0] // (batch * nt)
        assert w_rows * batch * nt == wout_f32.shape[0] and w_rows % 16 == 0
        w_block = pl.BlockSpec((w_rows, wout_f32.shape[1]), lambda b, t: (b * nt + t, 0))
        in_specs.append(w_block)
        args.append(wout_f32)
    out_specs = [
        pl.BlockSpec((tt, D_MODEL), lambda b, t: (b * nt + t, 0)),
        pl.BlockSpec((1, HIST_ROWS, C_CONV), lambda b, t: (b, 0, 0)),
    ]
    out_shape = [
        jax.ShapeDtypeStruct((batch * seq, D_MODEL), BF16),
        jax.ShapeDtypeStruct((batch, HIST_ROWS, C_CONV), F32),
    ]
    if emit_vn:
        out_specs.append(pl.BlockSpec((1, tt, C_GMLP), lambda b, t: (b, t, 0)))
        out_shape.append(jax.ShapeDtypeStruct((batch, seq, C_GMLP), F32))
    if cast_wout:
        out_specs.append(w_block)
        out_shape.append(jax.ShapeDtypeStruct(wout_f32.shape, BF16))
    return pl.pallas_call(
        functools.partial(_mixer_kernel, tt=tt, nt=nt, from_cache=from_cache, emit_vn=emit_vn, cast_wout=cast_wout),
        grid=(batch, nt),
        in_specs=in_specs,
        out_specs=out_specs,
        out_shape=out_shape,
        scratch_shapes=[pltpu.VMEM((CONV_TILES, HIST_ROWS + tt, LANES), F32)],
        compiler_params=pltpu.CompilerParams(
            dimension_semantics=("arbitrary", "arbitrary"), vmem_limit_bytes=VMEM_LIMIT_BYTES),
        name="mixer_cache" if from_cache else "mixer",
    )(*args)


def _outproj_kernel(m_ref, w_ref, x_ref, g_ref, y_ref, *, nj, tn):
    j = pl.program_id(1)
    y_ref[:, pl.ds(pl.multiple_of(j * tn, tn), tn)] = jnp.dot(m_ref[...], w_ref[...], preferred_element_type=F32)

    @pl.when(j == nj - 1)
    def _():
        def body(rows):
            out = y_ref[rows, :]
            y_ref[rows, :] = x_ref[rows, :] + out * _rms_scale(out) * g_ref[...]

        _row_groups(y_ref.shape[0], ROW_CHUNK, body)


def _outproj(mixed, w_bf16, x2d, g, *, tm, tn):
    m, d = x2d.shape
    nj = d // tn
    return pl.pallas_call(
        functools.partial(_outproj_kernel, nj=nj, tn=tn),
        grid=(m // tm, nj),
        in_specs=[
            pl.BlockSpec((tm, d), lambda i, j: (i, 0)),
            pl.BlockSpec((d, tn), lambda i, j: (0, j)),
            pl.BlockSpec((tm, d), lambda i, j: (i, 0)),
            pl.BlockSpec((1, d), lambda i, j: (0, 0)),
        ],
        out_specs=pl.BlockSpec((tm, d), lambda i, j: (i, 0)),
        out_shape=jax.ShapeDtypeStruct((m, d), F32),
        compiler_params=pltpu.CompilerParams(
            dimension_semantics=("parallel", "arbitrary"), vmem_limit_bytes=VMEM_LIMIT_BYTES),
        name="outproj",
    )(mixed, w_bf16, x2d, g)


def _layer(x, memk, memv, mem_layer, hist, lw, *, tt, emit_vn, cast_weights):
    batch, seq, d = x.shape
    x2d = x.reshape(batch * seq, d)
    if cast_weights:
        p, w_in_b = _inproj(x2d, lw["g_pre"], lw["w_in"], tm=batch * seq, tn=512, emit_w_bf16=True)
    else:
        (p,), w_in_b = _inproj(x2d, lw["g_pre"], lw["w_in"], tm=1024, tn=512), lw["w_in"]
    outs = list(_mixer(p, memk, memv, mem_layer, hist, lw["mixer"], lw["w_out"] if cast_weights else None,
                       batch=batch, seq=seq, tt=tt, emit_vn=emit_vn))
    w_out_b = outs.pop() if cast_weights else lw["w_out"]
    y = _outproj(outs[0], w_out_b, x2d, lw["g_post"], tm=512, tn=512)
    return (y.reshape(batch, seq, d),) + tuple(outs[1:]) + (w_in_b, w_out_b)


def kernel(x_prompt, x_sample, mem_prompt, cache_mem_k, cache_mem_v, cache_conv, g_pre, w_in, conv_w, conv_b,
           ln_conv_g, ln_conv_b, ln_v_g, ln_v_b, w_spatial, b_spatial, g_mem, w_mk, w_mv, g_branch, w_out, g_post):
    depth = w_in.shape[0]
    yp, ys = x_prompt, x_sample
    batch = x_prompt.shape[0]
    mk_l, mv_l, cp_l, cs_l, gv_l = [], [], [], [], []
    for l in range(depth):
        row = lambda a: a[l].reshape(1, -1)
        lw = {
            "g_pre": row(g_pre),
            "w_in": w_in[l],
            "w_out": w_out[l],
            "g_post": row(g_post),
            "mixer": (
                conv_w[l].reshape(CONV_W, CONV_TILES, LANES).transpose(1, 0, 2),
                conv_b[l].reshape(CONV_TILES, 1, LANES),
                row(ln_conv_g), row(ln_conv_b), row(ln_v_g), row(ln_v_b),
                w_spatial[l], b_spatial[l].T, row(g_branch),
            ),
        }
        hist = jnp.pad(cache_conv[l], ((0, 0), (HIST_PAD, 0), (0, 0)))
        dec_batch = x_sample.shape[0]
        ys, cs, gv, w_in_b, w_out_b = _layer(ys, cache_mem_k[l].reshape(dec_batch, N_MEM, C_X),
                                             cache_mem_v[l].reshape(dec_batch, N_MEM, C_X), 0, hist, lw,
                                             tt=x_sample.shape[1], emit_vn=True, cast_weights=True)
        mk, mv, mk_b, mv_b = _memkv(mem_prompt.reshape(batch * N_MEM, D_MODEL), row(g_mem), w_mk[l], w_mv[l], tn=256)
        yp, cp, _, _ = _layer(yp, mk_b.reshape(batch, N_MEM, C_X), mv_b.reshape(batch, N_MEM, C_X), 0, None,
                              dict(lw, w_in=w_in_b, w_out=w_out_b), tt=GMLP_CHUNK, emit_vn=False, cast_weights=False)
        mk_l.append(mk.reshape(batch, N_MEM, X_HEADS, X_HEAD_DIM))
        mv_l.append(mv.reshape(batch, N_MEM, X_HEADS, X_HEAD_DIM))
        cp_l.append(cp[:, HIST_PAD:, :])
        cs_l.append(cs[:, HIST_PAD:, :])
        gv_l.append(gv)
    return (yp, ys, jnp.stack(mk_l), jnp.stack(mv_l), jnp.stack(cp_l), jnp.stack(cs_l), jnp.stack(gv_l))
```

```python
import functools

import jax
import jax.numpy as jnp
from jax import lax
from jax.experimental import pallas as pl
from jax.experimental.pallas import tpu as pltpu

EPS = 1e-6
D_MODEL = 4096
C_CONV = 1536
C_GMLP = 1536
C_X = 1024
CONV_W = 31
G_HEADS = 4
G_HEAD_DIM = C_GMLP // G_HEADS
X_HEADS = 4
X_HEAD_DIM = C_X // X_HEADS
N_MEM = 256
GMLP_CHUNK = 128
N_IN = 3 * C_CONV + 3 * C_GMLP + 2 * C_X

OFF_A, OFF_BG, OFF_GC = 0, C_CONV, 2 * C_CONV
OFF_U, OFF_V, OFF_GG = 3 * C_CONV, 3 * C_CONV + C_GMLP, 3 * C_CONV + 2 * C_GMLP
OFF_Q = 3 * C_CONV + 3 * C_GMLP
OFF_GX = OFF_Q + C_X

LANES = 128
SUBLANES = 8
CONV_TILES = C_CONV // LANES
HIST_ROWS = 32
HIST_PAD = HIST_ROWS - (CONV_W - 1)
CONV_CHAINS = 4
V7X_VMEM_BYTES = 64 * 1024 * 1024
VMEM_LIMIT_BYTES = V7X_VMEM_BYTES - 8 * 1024 * 1024
ROW_CHUNK = 64

BF16 = jnp.bfloat16
F32 = jnp.float32


def _rms_scale(x):
    return lax.rsqrt(jnp.mean(x * x, axis=-1, keepdims=True) + EPS)


def _layer_norm(x, g, b):
    xc = x - jnp.mean(x, axis=-1, keepdims=True)
    return xc * lax.rsqrt(jnp.mean(xc * xc, axis=-1, keepdims=True) + EPS) * g + b


def _sigmoid(x):
    return 0.5 * jnp.tanh(0.5 * x) + 0.5


def _silu(x):
    return x * _sigmoid(x)


def _exact_zero_of(x):
    bits = lax.bitcast_convert_type(x, jnp.uint32)
    bits = lax.shift_right_logical(lax.shift_right_logical(bits, jnp.uint32(16)), jnp.uint32(16))
    return lax.bitcast_convert_type(bits, F32)


def _row_groups(n_rows, group, body):
    def step(r, carry):
        base = pl.multiple_of(r * ROW_CHUNK, ROW_CHUNK)
        for s in range(ROW_CHUNK // group):
            body(pl.ds(base + s * group, group))
        return carry

    lax.fori_loop(0, n_rows // ROW_CHUNK, step, 0)


def _rms_rows_to_bf16(x_ref, g_ref, h_ref):
    def body(rows):
        x = x_ref[rows, :]
        h_ref[rows, :] = (x * _rms_scale(x) * g_ref[...]).astype(BF16)

    _row_groups(x_ref.shape[0], 2 * SUBLANES, body)


def _inproj_kernel(x_ref, g_ref, w_ref, o_ref, *rest, emit_w_bf16):
    h_ref = rest[-1]

    @pl.when(pl.program_id(1) == 0)
    def _():
        _rms_rows_to_bf16(x_ref, g_ref, h_ref)

    w = w_ref[...].astype(BF16)
    if emit_w_bf16:
        rest[0][...] = w
    o_ref[...] = jnp.dot(h_ref[...], w, preferred_element_type=F32).astype(o_ref.dtype)


def _inproj(x2d, g, w, *, tm, tn, emit_w_bf16=False):
    m, d = x2d.shape
    n = w.shape[1]
    out_specs = [pl.BlockSpec((tm, tn), lambda i, j: (i, j))]
    out_shape = [jax.ShapeDtypeStruct((m, n), BF16)]
    x_mode = {}
    if emit_w_bf16:
        assert m == tm, "the bf16 weight copy is written once per column block, by the only row tile"
        out_specs.append(pl.BlockSpec((d, tn), lambda i, j: (0, j)))
        out_shape.append(jax.ShapeDtypeStruct((d, n), BF16))
        x_mode = dict(pipeline_mode=pl.Buffered(1))
    return pl.pallas_call(
        functools.partial(_inproj_kernel, emit_w_bf16=emit_w_bf16),
        grid=(m // tm, n // tn),
        in_specs=[
            pl.BlockSpec((tm, d), lambda i, j: (i, 0), **x_mode),
            pl.BlockSpec((1, d), lambda i, j: (0, 0)),
            pl.BlockSpec((d, tn), lambda i, j: (0, j)),
        ],
        out_specs=out_specs,
        out_shape=out_shape,
        scratch_shapes=[pltpu.VMEM((tm, d), BF16)],
        compiler_params=pltpu.CompilerParams(
            dimension_semantics=("parallel", "arbitrary"), vmem_limit_bytes=VMEM_LIMIT_BYTES),
        name="inproj_wcast" if emit_w_bf16 else "inproj",
    )(x2d, g, w)


def _memkv_kernel(mem_ref, g_ref, wk_ref, wv_ref, k_ref, v_ref, kb_ref, vb_ref, m_ref):
    @pl.when(pl.program_id(0) == 0)
    def _():
        _rms_rows_to_bf16(mem_ref, g_ref, m_ref)

    k = jnp.dot(m_ref[...], wk_ref[...].astype(BF16), preferred_element_type=F32)
    v = jnp.dot(m_ref[...], wv_ref[...].astype(BF16), preferred_element_type=F32)
    k_ref[...] = k
    v_ref[...] = v
    kb_ref[...] = k.astype(BF16)
    vb_ref[...] = v.astype(BF16)


def _memkv(mem2d, g, w_mk, w_mv, *, tn):
    m, d = mem2d.shape
    n = w_mk.shape[1]
    col = lambda j: (0, j)
    return pl.pallas_call(
        _memkv_kernel,
        grid=(n // tn,),
        in_specs=[
            pl.BlockSpec((m, d), lambda j: (0, 0)),
            pl.BlockSpec((1, d), lambda j: (0, 0)),
            pl.BlockSpec((d, tn), col),
            pl.BlockSpec((d, tn), col),
        ],
        out_specs=[pl.BlockSpec((m, tn), col)] * 4,
        out_shape=[jax.ShapeDtypeStruct((m, n), F32)] * 2 + [jax.ShapeDtypeStruct((m, n), BF16)] * 2,
        scratch_shapes=[pltpu.VMEM((m, d), BF16)],
        compiler_params=pltpu.CompilerParams(
            dimension_semantics=("arbitrary",), vmem_limit_bytes=VMEM_LIMIT_BYTES),
        name="memkv",
    )(mem2d, g, w_mk, w_mv)


def _mem_head(ref, h):
    if len(ref.shape) == 3:
        return ref[0, :, h * X_HEAD_DIM:(h + 1) * X_HEAD_DIM]
    return ref[0, 0, :, h, :]


_MIXER_INPUTS = ("p", "memk", "memv", "hist", "cw", "cb", "lcg", "lcb", "lvg", "lvb", "ws", "bs", "gb", "wout_f32")


def _mixer_kernel(*refs, tt, nt, from_cache, emit_vn, cast_wout):
    absent = set() if from_cache else {"hist"}
    absent |= set() if cast_wout else {"wout_f32"}
    names = [n for n in _MIXER_INPUTS if n not in absent]
    names += ["mixed", "conv"] + (["vn"] if emit_vn else []) + (["wout_b"] if cast_wout else []) + ["s"]
    r = dict(zip(names, refs))
    if cast_wout:
        r["wout_b"][...] = r["wout_f32"][...].astype(BF16)
    p_ref = r["p"]
    mixed_ref = r["mixed"]
    gb_ref = r["gb"]
    s_ref = r["s"]
    pf = lambda off, width: p_ref[:, off:off + width].astype(F32)

    t = pl.program_id(1)

    @pl.when(t == 0)
    def _():
        for c in range(CONV_TILES):
            if from_cache:
                s_ref[c, 0:HIST_ROWS, :] = r["hist"][0, :, c * LANES:(c + 1) * LANES]
            else:
                s_ref[c, 0:HIST_ROWS, :] = jnp.zeros((HIST_ROWS, LANES), F32)

    if nt > 1:
        @pl.when(t > 0)
        def _():
            for c in range(CONV_TILES):
                s_ref[c, 0:HIST_ROWS, :] = s_ref[c, tt:tt + HIST_ROWS, :]

    glu = pf(OFF_A, C_CONV) * _sigmoid(pf(OFF_BG, C_CONV))
    for c in range(CONV_TILES):
        s_ref[c, HIST_ROWS:HIST_ROWS + tt, :] = glu[:, c * LANES:(c + 1) * LANES]

    conv_tiles, chain_tails = [], []
    for c in range(CONV_TILES):
        w = r["cw"][c]
        taps = [jnp.broadcast_to(w[k:k + 1, :], (SUBLANES, LANES)) for k in range(CONV_W)]
        bias = jnp.broadcast_to(r["cb"][c], (SUBLANES, LANES))
        groups = []
        for g in range(tt // SUBLANES):
            acc = bias if len(chain_tails) < CONV_CHAINS else bias + _exact_zero_of(chain_tails[-CONV_CHAINS])
            for k in range(CONV_W):
                lo = g * SUBLANES + k + HIST_PAD
                acc = acc + taps[k] * s_ref[c, lo:lo + SUBLANES, :]
            groups.append(acc)
            chain_tails.append(acc)
        conv_tiles.append(jnp.concatenate(groups, axis=0))
    conv = jnp.concatenate(conv_tiles, axis=-1)

    @pl.when(t == nt - 1)
    def _():
        for c in range(CONV_TILES):
            r["conv"][0, :, c * LANES:(c + 1) * LANES] = s_ref[c, tt:tt + HIST_ROWS, :]

    yc = _silu(_layer_norm(conv, r["lcg"][...], r["lcb"][...])) * _silu(pf(OFF_GC, C_CONV))
    mixed_ref[:, 0:C_CONV] = (yc * _rms_scale(yc) * gb_ref[:, 0:C_CONV]).astype(BF16)

    vn = _layer_norm(pf(OFF_V, C_GMLP), r["lvg"][...], r["lvb"][...])
    if emit_vn:
        r["vn"][0] = vn
    vn_b = vn.astype(BF16)
    row = lax.broadcasted_iota(jnp.int32, (GMLP_CHUNK, GMLP_CHUNK), 0)
    col = lax.broadcasted_iota(jnp.int32, (GMLP_CHUNK, GMLP_CHUNK), 1)
    yg_heads = []
    for h in range(G_HEADS):
        lanes = slice(h * G_HEAD_DIM, (h + 1) * G_HEAD_DIM)
        wm = jnp.where(row >= col, r["ws"][h], 0.0).astype(BF16)[:tt, :tt]
        z = jnp.dot(wm, vn_b[:, lanes], preferred_element_type=F32) + r["bs"][0:tt, h:h + 1]
        yg_heads.append(pf(OFF_U + h * G_HEAD_DIM, G_HEAD_DIM) * z * _silu(pf(OFF_GG + h * G_HEAD_DIM, G_HEAD_DIM)))
    yg = jnp.concatenate(yg_heads, axis=-1)
    mixed_ref[:, C_CONV:C_CONV + C_GMLP] = (yg * _rms_scale(yg) * gb_ref[:, C_CONV:C_CONV + C_GMLP]).astype(BF16)

    scores = []
    for h in range(X_HEADS):
        q = (pf(OFF_Q + h * X_HEAD_DIM, X_HEAD_DIM) * (X_HEAD_DIM ** -0.5)).astype(BF16)
        k = _mem_head(r["memk"], h).astype(BF16)
        scores.append(lax.dot_general(q, k, (((1,), (1,)), ((), ())), preferred_element_type=F32))
    probs = []
    for s in scores:
        e = jnp.exp(s - jnp.max(s, axis=-1, keepdims=True))
        probs.append((e / jnp.sum(e, axis=-1, keepdims=True)).astype(BF16))
    yx_heads = []
    for h in range(X_HEADS):
        o = jnp.dot(probs[h], _mem_head(r["memv"], h).astype(BF16), preferred_element_type=F32)
        yx_heads.append(o * _silu(pf(OFF_GX + h * X_HEAD_DIM, X_HEAD_DIM)))
    yx = jnp.concatenate(yx_heads, axis=-1)
    mixed_ref[:, C_CONV + C_GMLP:D_MODEL] = (yx * _rms_scale(yx) * gb_ref[:, C_CONV + C_GMLP:D_MODEL]).astype(BF16)


def _mixer(p2d, memk, memv, mem_layer, hist, weights, wout_f32, *, batch, seq, tt, emit_vn):
    nt = seq // tt
    from_cache = hist is not None
    cast_wout = wout_f32 is not None
    const = lambda a: pl.BlockSpec(a.shape, lambda b, t, _n=a.ndim: (0,) * _n)
    in_specs = [pl.BlockSpec((tt, N_IN), lambda b, t: (b * nt + t, 0))]
    for mem in (memk, memv):
        if mem.ndim == 3:
            in_specs.append(pl.BlockSpec((1, N_MEM, C_X), lambda b, t: (b, 0, 0)))
        else:
            in_specs.append(pl.BlockSpec((1, 1, N_MEM, X_HEADS, X_HEAD_DIM), lambda b, t: (mem_layer, b, 0, 0, 0)))
    args = [p2d, memk, memv]
    if from_cache:
        in_specs.append(pl.BlockSpec((1, HIST_ROWS, C_CONV), lambda b, t: (b, 0, 0)))
        args.append(hist)
    in_specs += [const(w) for w in weights]
    args += list(weights)
    if cast_wout:
        w_rows = wout_f32.shape[0] // (batch * nt)
        assert w_rows * batch * nt == wout_f32.shape[0] and w_rows % 16 == 0
        w_block = pl.BlockSpec((w_rows, wout_f32.shape[1]), lambda b, t: (b * nt + t, 0))
        in_specs.append(w_block)
        args.append(wout_f32)
    out_specs = [
        pl.BlockSpec((tt, D_MODEL), lambda b, t: (b * nt + t, 0)),
        pl.BlockSpec((1, HIST_ROWS, C_CONV), lambda b, t: (b, 0, 0)),
    ]
    out_shape = [
        jax.ShapeDtypeStruct((batch * seq, D_MODEL), BF16),
        jax.ShapeDtypeStruct((batch, HIST_ROWS, C_CONV), F32),
    ]
    if emit_vn:
        out_specs.append(pl.BlockSpec((1, tt, C_GMLP), lambda b, t: (b, t, 0)))
        out_shape.append(jax.ShapeDtypeStruct((batch, seq, C_GMLP), F32))
    if cast_wout:
        out_specs.append(w_block)
        out_shape.append(jax.ShapeDtypeStruct(wout_f32.shape, BF16))
    return pl.pallas_call(
        functools.partial(_mixer_kernel, tt=tt, nt=nt, from_cache=from_cache, emit_vn=emit_vn, cast_wout=cast_wout),
        grid=(batch, nt),
        in_specs=in_specs,
        out_specs=out_specs,
        out_shape=out_shape,
        scratch_shapes=[pltpu.VMEM((CONV_TILES, HIST_ROWS + tt, LANES), F32)],
        compiler_params=pltpu.CompilerParams(
            dimension_semantics=("arbitrary", "arbitrary"), vmem_limit_bytes=VMEM_LIMIT_BYTES),
        name="mixer_cache" if from_cache else "mixer",
    )(*args)


def _outproj_kernel(m_ref, w_ref, x_ref, g_ref, y_ref, *, nj, tn):
    j = pl.program_id(1)
    y_ref[:, pl.ds(pl.multiple_of(j * tn, tn), tn)] = jnp.dot(m_ref[...], w_ref[...], preferred_element_type=F32)

    @pl.when(j == nj - 1)
    def _():
        def body(rows):
            out = y_ref[rows, :]
            y_ref[rows, :] = x_ref[rows, :] + out * _rms_scale(out) * g_ref[...]

        _row_groups(y_ref.shape[0], ROW_CHUNK, body)


def _outproj(mixed, w_bf16, x2d, g, *, tm, tn):
    m, d = x2d.shape
    nj = d // tn
    return pl.pallas_call(
        functools.partial(_outproj_kernel, nj=nj, tn=tn),
        grid=(m // tm, nj),
        in_specs=[
            pl.BlockSpec((tm, d), lambda i, j: (i, 0)),
            pl.BlockSpec((d, tn), lambda i, j: (0, j)),
            pl.BlockSpec((tm, d), lambda i, j: (i, 0)),
            pl.BlockSpec((1, d), lambda i, j: (0, 0)),
        ],
        out_specs=pl.BlockSpec((tm, d), lambda i, j: (i, 0)),
        out_shape=jax.ShapeDtypeStruct((m, d), F32),
        compiler_params=pltpu.CompilerParams(
            dimension_semantics=("parallel", "arbitrary"), vmem_limit_bytes=VMEM_LIMIT_BYTES),
        name="outproj",
    )(mixed, w_bf16, x2d, g)


HALF = GMLP_CHUNK // 2
FUSED_TM = 512
FUSED_TN = 512
FUSED_CONV_CHAINS = 3
FUSED_VMEM_LIMIT = V7X_VMEM_BYTES - 3 * 1024 * 1024
_FUSED_INPUTS = ("p", "memk", "memv", "cw", "cb", "lcg", "lcb", "lvg", "lvb", "ws", "bs", "gb", "gpost", "wout", "x")


def _mix_half_and_project(proj_ref, mix_ref, r, j):
    p_ref, s_ref, gb_ref = r["p"], r["s"], r["gb"]
    pf = lambda off, width: p_ref[:, off:off + width].astype(F32)
    rows = pl.ds(pl.multiple_of(j * HALF, HALF), HALF)

    scores = []
    for h in range(X_HEADS):
        q = (pf(OFF_Q + h * X_HEAD_DIM, X_HEAD_DIM) * (X_HEAD_DIM ** -0.5)).astype(BF16)
        scores.append(lax.dot_general(q, _mem_head(r["memk"], h), (((1,), (1,)), ((), ())),
                                      preferred_element_type=F32))

    par = lax.rem(j, 2)
    half_rows = pl.ds(pl.multiple_of(par * HALF, HALF), HALF)
    vn = _layer_norm(pf(OFF_V, C_GMLP), r["lvg"][...], r["lvb"][...])
    r["vn"][half_rows, :] = vn.astype(BF16)
    vn_chunk = r["vn"][...]
    row = lax.broadcasted_iota(jnp.int32, (HALF, GMLP_CHUNK), 0) + par * HALF
    col = lax.broadcasted_iota(jnp.int32, (HALF, GMLP_CHUNK), 1)
    z_heads = []
    for h in range(G_HEADS):
        wm = jnp.where(row >= col, r["ws"][h, half_rows, :], 0.0).astype(BF16)
        z_heads.append(jnp.dot(wm, vn_chunk[:, h * G_HEAD_DIM:(h + 1) * G_HEAD_DIM], preferred_element_type=F32)
                       + r["bs"][half_rows, h:h + 1])

    yx_heads = []
    for h in range(X_HEADS):
        e = jnp.exp(scores[h] - jnp.max(scores[h], axis=-1, keepdims=True))
        prob = (e / jnp.sum(e, axis=-1, keepdims=True)).astype(BF16)
        o = jnp.dot(prob, _mem_head(r["memv"], h), preferred_element_type=F32)
        yx_heads.append(o * _silu(pf(OFF_GX + h * X_HEAD_DIM, X_HEAD_DIM)))
    yx = jnp.concatenate(yx_heads, axis=-1)
    mix_ref[rows, C_CONV + C_GMLP:D_MODEL] = (yx * _rms_scale(yx) * gb_ref[:, C_CONV + C_GMLP:D_MODEL]).astype(BF16)

    yg = jnp.concatenate(
        [pf(OFF_U + h * G_HEAD_DIM, G_HEAD_DIM) * z_heads[h] * _silu(pf(OFF_GG + h * G_HEAD_DIM, G_HEAD_DIM))
         for h in range(G_HEADS)], axis=-1)
    mix_ref[rows, C_CONV:C_CONV + C_GMLP] = (yg * _rms_scale(yg) * gb_ref[:, C_CONV:C_CONV + C_GMLP]).astype(BF16)

    glu = pf(OFF_A, C_CONV) * _sigmoid(pf(OFF_BG, C_CONV))
    for c in range(CONV_TILES):
        s_ref[c, HIST_ROWS:HIST_ROWS + HALF, :] = glu[:, c * LANES:(c + 1) * LANES]
    conv_tiles, chain_tails = [], []
    for c in range(CONV_TILES):
        w = r["cw"][c]
        taps = [jnp.broadcast_to(w[k:k + 1, :], (SUBLANES, LANES)) for k in range(CONV_W)]
        bias = jnp.broadcast_to(r["cb"][c], (SUBLANES, LANES))
        groups = []
        for g in range(HALF // SUBLANES):
            acc = (bias if len(chain_tails) < FUSED_CONV_CHAINS
                   else bias + _exact_zero_of(chain_tails[-FUSED_CONV_CHAINS]))
            for k in range(CONV_W):
                lo = g * SUBLANES + k + HIST_PAD
                acc = acc + taps[k] * s_ref[c, lo:lo + SUBLANES, :]
            groups.append(acc)
            chain_tails.append(acc)
        conv_tiles.append(jnp.concatenate(groups, axis=0))
    conv = jnp.concatenate(conv_tiles, axis=-1)
    yc = _silu(_layer_norm(conv, r["lcg"][...], r["lcb"][...])) * _silu(pf(OFF_GC, C_CONV))
    mix_ref[rows, 0:C_CONV] = (yc * _rms_scale(yc) * gb_ref[:, 0:C_CONV]).astype(BF16)

    tn = FUSED_TN
    r["y"][:, pl.ds(pl.multiple_of(j * tn, tn), tn)] = jnp.dot(
        proj_ref[...], r["wout"][...], preferred_element_type=F32)


def _fused_kernel(*refs, n_tiles, halves_per_seq):
    names = list(_FUSED_INPUTS) + ["y", "conv", "s", "vn", "buf_a", "buf_b"]
    r = dict(zip(names, refs))
    s_ref = r["s"]
    nj = D_MODEL // FUSED_TN
    i = pl.program_id(0)
    j = pl.program_id(1)
    mix_tile = jnp.minimum(i, n_tiles - 1)
    t = lax.rem(mix_tile * nj + j, halves_per_seq)

    @pl.when(jnp.logical_and(i == 0, j == 0))
    def _():
        r["buf_b"][...] = jnp.zeros_like(r["buf_b"])
        r["vn"][...] = jnp.zeros_like(r["vn"])

    @pl.when(t == 0)
    def _():
        for c in range(CONV_TILES):
            s_ref[c, 0:HIST_ROWS, :] = jnp.zeros((HIST_ROWS, LANES), F32)

    @pl.when(t > 0)
    def _():
        for c in range(CONV_TILES):
            s_ref[c, 0:HIST_ROWS, :] = s_ref[c, HALF:HALF + HIST_ROWS, :]

    parity = lax.rem(i, 2)

    @pl.when(parity == 0)
    def _():
        _mix_half_and_project(r["buf_b"], r["buf_a"], r, j)

    @pl.when(parity == 1)
    def _():
        _mix_half_and_project(r["buf_a"], r["buf_b"], r, j)

    @pl.when(jnp.logical_and(t == halves_per_seq - 1, i < n_tiles))
    def _():
        for c in range(CONV_TILES):
            r["conv"][0, :, c * LANES:(c + 1) * LANES] = s_ref[c, HALF:HALF + HIST_ROWS, :]

    @pl.when(j == nj - 1)
    def _():
        y_ref, x_ref, g_ref = r["y"], r["x"], r["gpost"]

        def body(rows):
            out = y_ref[rows, :]
            y_ref[rows, :] = x_ref[rows, :] + out * _rms_scale(out) * g_ref[...]

        _row_groups(y_ref.shape[0], ROW_CHUNK, body)


def _fused_mix_project(p2d, memk, memv, weights, w_out_b, x2d, *, batch, seq):
    nj = D_MODEL // FUSED_TN
    assert FUSED_TM == nj * HALF and seq % FUSED_TM == 0
    n_tiles = batch * seq // FUSED_TM
    halves_per_seq = seq // HALF
    mix_half = lambda i, j: jnp.minimum(i, n_tiles - 1) * nj + j
    proj_tile = lambda i: jnp.maximum(i - 1, 0)
    const = lambda a: pl.BlockSpec(a.shape, lambda i, j, _n=a.ndim: (0,) * _n)
    in_specs = [
        pl.BlockSpec((HALF, N_IN), lambda i, j: (mix_half(i, j), 0)),
        pl.BlockSpec((1, N_MEM, C_X), lambda i, j: (mix_half(i, j) // halves_per_seq, 0, 0)),
        pl.BlockSpec((1, N_MEM, C_X), lambda i, j: (mix_half(i, j) // halves_per_seq, 0, 0)),
    ]
    in_specs += [const(w) for w in weights]
    in_specs += [
        pl.BlockSpec((D_MODEL, FUSED_TN), lambda i, j: (0, j)),
        pl.BlockSpec((FUSED_TM, D_MODEL), lambda i, j: (proj_tile(i), 0)),
    ]
    return pl.pallas_call(
        functools.partial(_fused_kernel, n_tiles=n_tiles, halves_per_seq=halves_per_seq),
        grid=(n_tiles + 1, nj),
        in_specs=in_specs,
        out_specs=[
            pl.BlockSpec((FUSED_TM, D_MODEL), lambda i, j: (proj_tile(i), 0)),
            pl.BlockSpec((1, HIST_ROWS, C_CONV), lambda i, j: (mix_half(i, j) // halves_per_seq, 0, 0)),
        ],
        out_shape=[
            jax.ShapeDtypeStruct((batch * seq, D_MODEL), F32),
            jax.ShapeDtypeStruct((batch, HIST_ROWS, C_CONV), F32),
        ],
        scratch_shapes=[
            pltpu.VMEM((CONV_TILES, HIST_ROWS + HALF, LANES), F32),
            pltpu.VMEM((GMLP_CHUNK, C_GMLP), BF16),
            pltpu.VMEM((FUSED_TM, D_MODEL), BF16),
            pltpu.VMEM((FUSED_TM, D_MODEL), BF16),
        ],
        compiler_params=pltpu.CompilerParams(
            dimension_semantics=("arbitrary", "arbitrary"), vmem_limit_bytes=FUSED_VMEM_LIMIT),
        name="mixproj",
    )(p2d, memk, memv, *weights, w_out_b, x2d)


def _layer(x, memk, memv, mem_layer, hist, lw, *, tt, emit_vn, cast_weights):
    batch, seq, d = x.shape
    x2d = x.reshape(batch * seq, d)
    if cast_weights:
        p, w_in_b = _inproj(x2d, lw["g_pre"], lw["w_in"], tm=batch * seq, tn=512, emit_w_bf16=True)
    else:
        (p,), w_in_b = _inproj(x2d, lw["g_pre"], lw["w_in"], tm=1024, tn=512), lw["w_in"]
    if hist is None and not emit_vn and not cast_weights and memk.ndim == 3 and memk.dtype == BF16:
        y, conv = _fused_mix_project(p, memk, memv, lw["mixer"] + (lw["g_post"],), lw["w_out"], x2d,
                                     batch=batch, seq=seq)
        return y.reshape(batch, seq, d), conv, w_in_b, lw["w_out"]
    outs = list(_mixer(p, memk, memv, mem_layer, hist, lw["mixer"], lw["w_out"] if cast_weights else None,
                       batch=batch, seq=seq, tt=tt, emit_vn=emit_vn))
    w_out_b = outs.pop() if cast_weights else lw["w_out"]
    y = _outproj(outs[0], w_out_b, x2d, lw["g_post"], tm=512, tn=512)
    return (y.reshape(batch, seq, d),) + tuple(outs[1:]) + (w_in_b, w_out_b)


def kernel(x_prompt, x_sample, mem_prompt, cache_mem_k, cache_mem_v, cache_conv, g_pre, w_in, conv_w, conv_b,
           ln_conv_g, ln_conv_b, ln_v_g, ln_v_b, w_spatial, b_spatial, g_mem, w_mk, w_mv, g_branch, w_out, g_post):
    depth = w_in.shape[0]
    yp, ys = x_prompt, x_sample
    batch = x_prompt.shape[0]
    mk_l, mv_l, cp_l, cs_l, gv_l = [], [], [], [], []
    for l in range(depth):
        row = lambda a: a[l].reshape(1, -1)
        lw = {
            "g_pre": row(g_pre),
            "w_in": w_in[l],
            "w_out": w_out[l],
            "g_post": row(g_post),
            "mixer": (
                conv_w[l].reshape(CONV_W, CONV_TILES, LANES).transpose(1, 0, 2),
                conv_b[l].reshape(CONV_TILES, 1, LANES),
                row(ln_conv_g), row(ln_conv_b), row(ln_v_g), row(ln_v_b),
                w_spatial[l], b_spatial[l].T, row(g_branch),
            ),
        }
        hist = jnp.pad(cache_conv[l], ((0, 0), (HIST_PAD, 0), (0, 0)))
        dec_batch = x_sample.shape[0]
        ys, cs, gv, w_in_b, w_out_b = _layer(ys, cache_mem_k[l].reshape(dec_batch, N_MEM, C_X),
                                             cache_mem_v[l].reshape(dec_batch, N_MEM, C_X), 0, hist, lw,
                                             tt=x_sample.shape[1], emit_vn=True, cast_weights=True)
        mk, mv, mk_b, mv_b = _memkv(mem_prompt.reshape(batch * N_MEM, D_MODEL), row(g_mem), w_mk[l], w_mv[l], tn=256)
        yp, cp, _, _ = _layer(yp, mk_b.reshape(batch, N_MEM, C_X), mv_b.reshape(batch, N_MEM, C_X), 0, None,
                              dict(lw, w_in=w_in_b, w_out=w_out_b), tt=GMLP_CHUNK, emit_vn=False, cast_weights=False)
        mk_l.append(mk.reshape(batch, N_MEM, X_HEADS, X_HEAD_DIM))
        mv_l.append(mv.reshape(batch, N_MEM, X_HEADS, X_HEAD_DIM))
        cp_l.append(cp[:, HIST_PAD:, :])
        cs_l.append(cs[:, HIST_PAD:, :])
        gv_l.append(gv)
    return (yp, ys, jnp.stack(mk_l), jnp.stack(mv_l), jnp.stack(cp_l), jnp.stack(cs_l), jnp.stack(gv_l))
```

```python
import functools

import jax
import jax.numpy as jnp
from jax import lax
from jax.experimental import pallas as pl
from jax.experimental.pallas import tpu as pltpu

EPS = 1e-6
D_MODEL = 4096
C_CONV = 1536
C_GMLP = 1536
C_X = 1024
CONV_W = 31
G_HEADS = 4
G_HEAD_DIM = C_GMLP // G_HEADS
X_HEADS = 4
X_HEAD_DIM = C_X // X_HEADS
N_MEM = 256
GMLP_CHUNK = 128
N_IN = 3 * C_CONV + 3 * C_GMLP + 2 * C_X

OFF_A, OFF_BG, OFF_GC = 0, C_CONV, 2 * C_CONV
OFF_U, OFF_V, OFF_GG = 3 * C_CONV, 3 * C_CONV + C_GMLP, 3 * C_CONV + 2 * C_GMLP
OFF_Q = 3 * C_CONV + 3 * C_GMLP
OFF_GX = OFF_Q + C_X

LANES = 128
SUBLANES = 8
CONV_TILES = C_CONV // LANES
HIST_ROWS = 32
HIST_PAD = HIST_ROWS - (CONV_W - 1)
CONV_CHAINS = 4
V7X_VMEM_BYTES = 64 * 1024 * 1024
VMEM_LIMIT_BYTES = V7X_VMEM_BYTES - 8 * 1024 * 1024
ROW_CHUNK = 64

BF16 = jnp.bfloat16
F32 = jnp.float32


def _rms_scale(x):
    return lax.rsqrt(jnp.mean(x * x, axis=-1, keepdims=True) + EPS)


def _layer_norm(x, g, b):
    xc = x - jnp.mean(x, axis=-1, keepdims=True)
    return xc * lax.rsqrt(jnp.mean(xc * xc, axis=-1, keepdims=True) + EPS) * g + b


def _sigmoid(x):
    return 0.5 * jnp.tanh(0.5 * x) + 0.5


def _silu(x):
    return x * _sigmoid(x)


def _exact_zero_of(x):
    bits = lax.bitcast_convert_type(x, jnp.uint32)
    bits = lax.shift_right_logical(lax.shift_right_logical(bits, jnp.uint32(16)), jnp.uint32(16))
    return lax.bitcast_convert_type(bits, F32)


def _row_groups(n_rows, group, body):
    def step(r, carry):
        base = pl.multiple_of(r * ROW_CHUNK, ROW_CHUNK)
        for s in range(ROW_CHUNK // group):
            body(pl.ds(base + s * group, group))
        return carry

    lax.fori_loop(0, n_rows // ROW_CHUNK, step, 0)


def _rms_rows_to_bf16(x_ref, g_ref, h_ref):
    def body(rows):
        x = x_ref[rows, :]
        h_ref[rows, :] = (x * _rms_scale(x) * g_ref[...]).astype(BF16)

    _row_groups(x_ref.shape[0], 2 * SUBLANES, body)


def _residual_rms_in_place(y_ref, x_ref, g_ref):
    def body(rows):
        out = y_ref[rows, :]
        y_ref[rows, :] = x_ref[rows, :] + out * _rms_scale(out) * g_ref[...]

    _row_groups(y_ref.shape[0], ROW_CHUNK, body)


def _inproj_kernel(x_ref, g_ref, w_ref, o_ref, *rest, emit_w_bf16):
    h_ref = rest[-1]

    @pl.when(pl.program_id(1) == 0)
    def _():
        _rms_rows_to_bf16(x_ref, g_ref, h_ref)

    w = w_ref[...].astype(BF16)
    if emit_w_bf16:
        rest[0][...] = w
    o_ref[...] = jnp.dot(h_ref[...], w, preferred_element_type=F32).astype(o_ref.dtype)


def _inproj(x2d, g, w, *, tm, tn, emit_w_bf16=False):
    m, d = x2d.shape
    n = w.shape[1]
    out_specs = [pl.BlockSpec((tm, tn), lambda i, j: (i, j))]
    out_shape = [jax.ShapeDtypeStruct((m, n), BF16)]
    x_mode = {}
    if emit_w_bf16:
        assert m == tm, "the bf16 weight copy is written once per column block, by the only row tile"
        out_specs.append(pl.BlockSpec((d, tn), lambda i, j: (0, j)))
        out_shape.append(jax.ShapeDtypeStruct((d, n), BF16))
        x_mode = dict(pipeline_mode=pl.Buffered(1))
    return pl.pallas_call(
        functools.partial(_inproj_kernel, emit_w_bf16=emit_w_bf16),
        grid=(m // tm, n // tn),
        in_specs=[
            pl.BlockSpec((tm, d), lambda i, j: (i, 0), **x_mode),
            pl.BlockSpec((1, d), lambda i, j: (0, 0)),
            pl.BlockSpec((d, tn), lambda i, j: (0, j)),
        ],
        out_specs=out_specs,
        out_shape=out_shape,
        scratch_shapes=[pltpu.VMEM((tm, d), BF16)],
        compiler_params=pltpu.CompilerParams(
            dimension_semantics=("parallel", "arbitrary"), vmem_limit_bytes=VMEM_LIMIT_BYTES),
        name="inproj_wcast" if emit_w_bf16 else "inproj",
    )(x2d, g, w)


STREAM_TM = 1024
STREAM_TN = 512
STREAM_CHUNK = 64


def _inproj_stream_kernel(x0_ref, xs_ref, g_ref, w_ref, o_ref, h_a, h_b, *, n_chunks):
    i, j = pl.program_id(0), pl.program_id(1)

    @pl.when(jnp.logical_and(i == 0, j == 0))
    def _():
        _rms_rows_to_bf16(x0_ref, g_ref, h_a)

    def step(h_cur, h_next):
        chunk = jnp.minimum(j, n_chunks - 1)
        group = 2 * SUBLANES
        for s in range(STREAM_CHUNK // group):
            x = xs_ref[s * group:(s + 1) * group, :]
            rows = pl.ds(pl.multiple_of(chunk * STREAM_CHUNK, STREAM_CHUNK) + s * group, group)
            h_next[rows, :] = (x * _rms_scale(x) * g_ref[...]).astype(BF16)
        o_ref[...] = jnp.dot(h_cur[...], w_ref[...], preferred_element_type=F32).astype(o_ref.dtype)

    parity = lax.rem(i, 2)

    @pl.when(parity == 0)
    def _():
        step(h_a, h_b)

    @pl.when(parity == 1)
    def _():
        step(h_b, h_a)


def _inproj_stream(x2d, g, w_bf16):
    m, d = x2d.shape
    n = w_bf16.shape[1]
    tm, tn = STREAM_TM, STREAM_TN
    n_tiles, n_chunks = m // tm, tm // STREAM_CHUNK
    assert m % tm == 0 and n % tn == 0 and n // tn >= n_chunks

    def next_rows(i, j):
        return (jnp.minimum(i + 1, n_tiles - 1) * n_chunks + jnp.minimum(j, n_chunks - 1), 0)

    return pl.pallas_call(
        functools.partial(_inproj_stream_kernel, n_chunks=n_chunks),
        grid=(n_tiles, n // tn),
        in_specs=[
            pl.BlockSpec((tm, d), lambda i, j: (0, 0), pipeline_mode=pl.Buffered(1)),
            pl.BlockSpec((STREAM_CHUNK, d), next_rows),
            pl.BlockSpec((1, d), lambda i, j: (0, 0)),
            pl.BlockSpec((d, tn), lambda i, j: (0, j)),
        ],
        out_specs=pl.BlockSpec((tm, tn), lambda i, j: (i, j)),
        out_shape=jax.ShapeDtypeStruct((m, n), BF16),
        scratch_shapes=[pltpu.VMEM((tm, d), BF16), pltpu.VMEM((tm, d), BF16)],
        compiler_params=pltpu.CompilerParams(
            dimension_semantics=("arbitrary", "arbitrary"), vmem_limit_bytes=VMEM_LIMIT_BYTES),
        name="inproj_stream",
    )(x2d, x2d, g, w_bf16)


def _memkv_kernel(mem_ref, g_ref, wk_ref, wv_ref, k_ref, v_ref, kb_ref, vb_ref, m_ref):
    @pl.when(pl.program_id(0) == 0)
    def _():
        _rms_rows_to_bf16(mem_ref, g_ref, m_ref)

    k = jnp.dot(m_ref[...], wk_ref[...].astype(BF16), preferred_element_type=F32)
    v = jnp.dot(m_ref[...], wv_ref[...].astype(BF16), preferred_element_type=F32)
    k_ref[...] = k
    v_ref[...] = v
    kb_ref[...] = k.astype(BF16)
    vb_ref[...] = v.astype(BF16)


def _memkv(mem2d, g, w_mk, w_mv, *, tn):
    m, d = mem2d.shape
    n = w_mk.shape[1]
    col = lambda j: (0, j)
    return pl.pallas_call(
        _memkv_kernel,
        grid=(n // tn,),
        in_specs=[
            pl.BlockSpec((m, d), lambda j: (0, 0)),
            pl.BlockSpec((1, d), lambda j: (0, 0)),
            pl.BlockSpec((d, tn), col),
            pl.BlockSpec((d, tn), col),
        ],
        out_specs=[pl.BlockSpec((m, tn), col)] * 4,
        out_shape=[jax.ShapeDtypeStruct((m, n), F32)] * 2 + [jax.ShapeDtypeStruct((m, n), BF16)] * 2,
        scratch_shapes=[pltpu.VMEM((m, d), BF16)],
        compiler_params=pltpu.CompilerParams(
            dimension_semantics=("arbitrary",), vmem_limit_bytes=VMEM_LIMIT_BYTES),
        name="memkv",
    )(mem2d, g, w_mk, w_mv)


def _mem_head(ref, h):
    return ref[0, :, h * X_HEAD_DIM:(h + 1) * X_HEAD_DIM]


_MIXER_INPUTS = ("p", "memk", "memv", "hist", "cw", "cb", "lcg", "lcb", "lvg", "lvb", "ws", "bs", "gb", "wout_f32")


def _mixer_kernel(*refs, tt, nt, from_cache, emit_vn, cast_wout):
    absent = set() if from_cache else {"hist"}
    absent |= set() if cast_wout else {"wout_f32"}
    names = [n for n in _MIXER_INPUTS if n not in absent]
    names += ["mixed", "conv"] + (["vn"] if emit_vn else []) + (["wout_b"] if cast_wout else []) + ["s"]
    r = dict(zip(names, refs))
    if cast_wout:
        r["wout_b"][...] = r["wout_f32"][...].astype(BF16)
    p_ref = r["p"]
    mixed_ref = r["mixed"]
    gb_ref = r["gb"]
    s_ref = r["s"]
    pf = lambda off, width: p_ref[:, off:off + width].astype(F32)

    t = pl.program_id(1)

    @pl.when(t == 0)
    def _():
        for c in range(CONV_TILES):
            if from_cache:
                s_ref[c, 0:HIST_ROWS, :] = r["hist"][0, :, c * LANES:(c + 1) * LANES]
            else:
                s_ref[c, 0:HIST_ROWS, :] = jnp.zeros((HIST_ROWS, LANES), F32)

    if nt > 1:
        @pl.when(t > 0)
        def _():
            for c in range(CONV_TILES):
                s_ref[c, 0:HIST_ROWS, :] = s_ref[c, tt:tt + HIST_ROWS, :]

    glu = pf(OFF_A, C_CONV) * _sigmoid(pf(OFF_BG, C_CONV))
    for c in range(CONV_TILES):
        s_ref[c, HIST_ROWS:HIST_ROWS + tt, :] = glu[:, c * LANES:(c + 1) * LANES]

    conv_tiles, chain_tails = [], []
    for c in range(CONV_TILES):
        w = r["cw"][c]
        taps = [jnp.broadcast_to(w[k:k + 1, :], (SUBLANES, LANES)) for k in range(CONV_W)]
        bias = jnp.broadcast_to(r["cb"][c], (SUBLANES, LANES))
        groups = []
        for g in range(tt // SUBLANES):
            acc = bias if len(chain_tails) < CONV_CHAINS else bias + _exact_zero_of(chain_tails[-CONV_CHAINS])
            for k in range(CONV_W):
                lo = g * SUBLANES + k + HIST_PAD
                acc = acc + taps[k] * s_ref[c, lo:lo + SUBLANES, :]
            groups.append(acc)
            chain_tails.append(acc)
        conv_tiles.append(jnp.concatenate(groups, axis=0))
    conv = jnp.concatenate(conv_tiles, axis=-1)

    @pl.when(t == nt - 1)
    def _():
        for c in range(CONV_TILES):
            r["conv"][0, :, c * LANES:(c + 1) * LANES] = s_ref[c, tt:tt + HIST_ROWS, :]

    yc = _silu(_layer_norm(conv, r["lcg"][...], r["lcb"][...])) * _silu(pf(OFF_GC, C_CONV))
    mixed_ref[:, 0:C_CONV] = (yc * _rms_scale(yc) * gb_ref[:, 0:C_CONV]).astype(BF16)

    vn = _layer_norm(pf(OFF_V, C_GMLP), r["lvg"][...], r["lvb"][...])
    if emit_vn:
        r["vn"][0] = vn
    vn_b = vn.astype(BF16)
    row = lax.broadcasted_iota(jnp.int32, (GMLP_CHUNK, GMLP_CHUNK), 0)
    col = lax.broadcasted_iota(jnp.int32, (GMLP_CHUNK, GMLP_CHUNK), 1)
    yg_heads = []
    for h in range(G_HEADS):
        lanes = slice(h * G_HEAD_DIM, (h + 1) * G_HEAD_DIM)
        wm = jnp.where(row >= col, r["ws"][h], 0.0).astype(BF16)[:tt, :tt]
        z = jnp.dot(wm, vn_b[:, lanes], preferred_element_type=F32) + r["bs"][0:tt, h:h + 1]
        yg_heads.append(pf(OFF_U + h * G_HEAD_DIM, G_HEAD_DIM) * z * _silu(pf(OFF_GG + h * G_HEAD_DIM, G_HEAD_DIM)))
    yg = jnp.concatenate(yg_heads, axis=-1)
    mixed_ref[:, C_CONV:C_CONV + C_GMLP] = (yg * _rms_scale(yg) * gb_ref[:, C_CONV:C_CONV + C_GMLP]).astype(BF16)

    scores = []
    for h in range(X_HEADS):
        q = (pf(OFF_Q + h * X_HEAD_DIM, X_HEAD_DIM) * (X_HEAD_DIM ** -0.5)).astype(BF16)
        k = _mem_head(r["memk"], h).astype(BF16)
        scores.append(lax.dot_general(q, k, (((1,), (1,)), ((), ())), preferred_element_type=F32))
    probs = []
    for s in scores:
        e = jnp.exp(s - jnp.max(s, axis=-1, keepdims=True))
        probs.append((e / jnp.sum(e, axis=-1, keepdims=True)).astype(BF16))
    yx_heads = []
    for h in range(X_HEADS):
        o = jnp.dot(probs[h], _mem_head(r["memv"], h).astype(BF16), preferred_element_type=F32)
        yx_heads.append(o * _silu(pf(OFF_GX + h * X_HEAD_DIM, X_HEAD_DIM)))
    yx = jnp.concatenate(yx_heads, axis=-1)
    mixed_ref[:, C_CONV + C_GMLP:D_MODEL] = (yx * _rms_scale(yx) * gb_ref[:, C_CONV + C_GMLP:D_MODEL]).astype(BF16)


def _mixer(p2d, memk, memv, mem_layer, hist, weights, wout_f32, *, batch, seq, tt, emit_vn):
    nt = seq // tt
    from_cache = hist is not None
    cast_wout = wout_f32 is not None
    const = lambda a: pl.BlockSpec(a.shape, lambda b, t, _n=a.ndim: (0,) * _n)
    in_specs = [pl.BlockSpec((tt, N_IN), lambda b, t: (b * nt + t, 0))]
    in_specs += [pl.BlockSpec((1, N_MEM, C_X), lambda b, t: (b, 0, 0))] * 2
    args = [p2d, memk, memv]
    if from_cache:
        in_specs.append(pl.BlockSpec((1, HIST_ROWS, C_CONV), lambda b, t: (b, 0, 0)))
        args.append(hist)
    in_specs += [const(w) for w in weights]
    args += list(weights)
    if cast_wout:
        w_rows = wout_f32.shape[0] // (batch * nt)
        assert w_rows * batch * nt == wout_f32.shape[0] and w_rows % 16 == 0
        w_block = pl.BlockSpec((w_rows, wout_f32.shape[1]), lambda b, t: (b * nt + t, 0))
        in_specs.append(w_block)
        args.append(wout_f32)
    out_specs = [
        pl.BlockSpec((tt, D_MODEL), lambda b, t: (b * nt + t, 0)),
        pl.BlockSpec((1, HIST_ROWS, C_CONV), lambda b, t: (b, 0, 0)),
    ]
    out_shape = [
        jax.ShapeDtypeStruct((batch * seq, D_MODEL), BF16),
        jax.ShapeDtypeStruct((batch, HIST_ROWS, C_CONV), F32),
    ]
    if emit_vn:
        out_specs.append(pl.BlockSpec((1, tt, C_GMLP), lambda b, t: (b, t, 0)))
        out_shape.append(jax.ShapeDtypeStruct((batch, seq, C_GMLP), F32))
    if cast_wout:
        out_specs.append(w_block)
        out_shape.append(jax.ShapeDtypeStruct(wout_f32.shape, BF16))
    return pl.pallas_call(
        functools.partial(_mixer_kernel, tt=tt, nt=nt, from_cache=from_cache, emit_vn=emit_vn, cast_wout=cast_wout),
        grid=(batch, nt),
        in_specs=in_specs,
        out_specs=out_specs,
        out_shape=out_shape,
        scratch_shapes=[pltpu.VMEM((CONV_TILES, HIST_ROWS + tt, LANES), F32)],
        compiler_params=pltpu.CompilerParams(
            dimension_semantics=("arbitrary", "arbitrary"), vmem_limit_bytes=VMEM_LIMIT_BYTES),
        name="mixer_cache" if from_cache else "mixer",
    )(*args)


def _outproj_kernel(m_ref, w_ref, x_ref, g_ref, y_ref, *, nj, tn):
    j = pl.program_id(1)
    y_ref[:, pl.ds(pl.multiple_of(j * tn, tn), tn)] = jnp.dot(m_ref[...], w_ref[...], preferred_element_type=F32)

    @pl.when(j == nj - 1)
    def _():
        _residual_rms_in_place(y_ref, x_ref, g_ref)


def _outproj(mixed, w_bf16, x2d, g, *, tm, tn):
    m, d = x2d.shape
    nj = d // tn
    return pl.pallas_call(
        functools.partial(_outproj_kernel, nj=nj, tn=tn),
        grid=(m // tm, nj),
        in_specs=[
            pl.BlockSpec((tm, d), lambda i, j: (i, 0)),
            pl.BlockSpec((d, tn), lambda i, j: (0, j)),
            pl.BlockSpec((tm, d), lambda i, j: (i, 0)),
            pl.BlockSpec((1, d), lambda i, j: (0, 0)),
        ],
        out_specs=pl.BlockSpec((tm, d), lambda i, j: (i, 0)),
        out_shape=jax.ShapeDtypeStruct((m, d), F32),
        compiler_params=pltpu.CompilerParams(
            dimension_semantics=("parallel", "arbitrary"), vmem_limit_bytes=VMEM_LIMIT_BYTES),
        name="outproj",
    )(mixed, w_bf16, x2d, g)


HALF = GMLP_CHUNK // 2
FUSED_TM = 512
FUSED_TN = 512
FUSED_SPLIT = 384
FUSED_CONV_CHAINS = 3
FUSED_VMEM_LIMIT = V7X_VMEM_BYTES - 3 * 1024 * 1024
_FUSED_INPUTS = ("p", "memk", "memv", "cw", "cb", "lcg", "lcb", "lvg", "lvb", "ws", "bs", "gb", "gpost", "wout", "x")


def _mix_half_and_project(proj_ref, mix_ref, r, j):
    def project(lo, hi):
        if proj_ref is not None:
            cols = pl.ds(pl.multiple_of(j * FUSED_TN, FUSED_TN), FUSED_TN)
            r["y"][lo:hi, cols] = jnp.dot(proj_ref[lo:hi, :], r["wout"][...], preferred_element_type=F32)

    if mix_ref is None:
        project(0, FUSED_TM)
        return

    p_ref, s_ref, gb_ref = r["p"], r["s"], r["gb"]
    pf = lambda off, width: p_ref[:, off:off + width].astype(F32)
    rows = pl.ds(pl.multiple_of(j * HALF, HALF), HALF)

    scores = []
    for h in range(X_HEADS):
        q = (pf(OFF_Q + h * X_HEAD_DIM, X_HEAD_DIM) * (X_HEAD_DIM ** -0.5)).astype(BF16)
        scores.append(lax.dot_general(q, _mem_head(r["memk"], h), (((1,), (1,)), ((), ())),
                                      preferred_element_type=F32))

    par = lax.rem(j, 2)
    half_rows = pl.ds(pl.multiple_of(par * HALF, HALF), HALF)
    vn = _layer_norm(pf(OFF_V, C_GMLP), r["lvg"][...], r["lvb"][...])
    r["vn"][half_rows, :] = vn.astype(BF16)
    vn_chunk = r["vn"][...]
    row = lax.broadcasted_iota(jnp.int32, (HALF, GMLP_CHUNK), 0) + par * HALF
    col = lax.broadcasted_iota(jnp.int32, (HALF, GMLP_CHUNK), 1)
    z_heads = []
    for h in range(G_HEADS):
        wm = jnp.where(row >= col, r["ws"][h, half_rows, :], 0.0).astype(BF16)
        z_heads.append(jnp.dot(wm, vn_chunk[:, h * G_HEAD_DIM:(h + 1) * G_HEAD_DIM], preferred_element_type=F32)
                       + r["bs"][half_rows, h:h + 1])

    project(0, FUSED_SPLIT)

    yx_heads = []
    for h in range(X_HEADS):
        e = jnp.exp(scores[h] - jnp.max(scores[h], axis=-1, keepdims=True))
        prob = (e / jnp.sum(e, axis=-1, keepdims=True)).astype(BF16)
        o = jnp.dot(prob, _mem_head(r["memv"], h), preferred_element_type=F32)
        yx_heads.append(o * _silu(pf(OFF_GX + h * X_HEAD_DIM, X_HEAD_DIM)))
    yx = jnp.concatenate(yx_heads, axis=-1)
    mix_ref[rows, C_CONV + C_GMLP:D_MODEL] = (yx * _rms_scale(yx) * gb_ref[:, C_CONV + C_GMLP:D_MODEL]).astype(BF16)

    yg = jnp.concatenate(
        [pf(OFF_U + h * G_HEAD_DIM, G_HEAD_DIM) * z_heads[h] * _silu(pf(OFF_GG + h * G_HEAD_DIM, G_HEAD_DIM))
         for h in range(G_HEADS)], axis=-1)
    mix_ref[rows, C_CONV:C_CONV + C_GMLP] = (yg * _rms_scale(yg) * gb_ref[:, C_CONV:C_CONV + C_GMLP]).astype(BF16)

    glu = pf(OFF_A, C_CONV) * _sigmoid(pf(OFF_BG, C_CONV))
    for c in range(CONV_TILES):
        s_ref[c, HIST_ROWS:HIST_ROWS + HALF, :] = glu[:, c * LANES:(c + 1) * LANES]
    conv_tiles, chain_tails = [], []
    for c in range(CONV_TILES):
        w = r["cw"][c]
        taps = [jnp.broadcast_to(w[k:k + 1, :], (SUBLANES, LANES)) for k in range(CONV_W)]
        bias = jnp.broadcast_to(r["cb"][c], (SUBLANES, LANES))
        groups = []
        for g in range(HALF // SUBLANES):
            acc = (bias if len(chain_tails) < FUSED_CONV_CHAINS
                   else bias + _exact_zero_of(chain_tails[-FUSED_CONV_CHAINS]))
            for k in range(CONV_W):
                lo = g * SUBLANES + k + HIST_PAD
                acc = acc + taps[k] * s_ref[c, lo:lo + SUBLANES, :]
            groups.append(acc)
            chain_tails.append(acc)
        conv_tiles.append(jnp.concatenate(groups, axis=0))
    conv = jnp.concatenate(conv_tiles, axis=-1)
    yc = _silu(_layer_norm(conv, r["lcg"][...], r["lcb"][...])) * _silu(pf(OFF_GC, C_CONV))
    mix_ref[rows, 0:C_CONV] = (yc * _rms_scale(yc) * gb_ref[:, 0:C_CONV]).astype(BF16)

    project(FUSED_SPLIT, FUSED_TM)


def _fused_kernel(*refs, n_tiles, halves_per_seq):
    names = list(_FUSED_INPUTS) + ["y", "conv", "s", "vn", "buf_a", "buf_b"]
    r = dict(zip(names, refs))
    s_ref = r["s"]
    nj = D_MODEL // FUSED_TN
    i = pl.program_id(0)
    j = pl.program_id(1)
    mixing = i < n_tiles
    projecting = i > 0
    t = lax.rem(jnp.minimum(i, n_tiles - 1) * nj + j, halves_per_seq)

    @pl.when(jnp.logical_and(i == 0, j == 0))
    def _():
        r["vn"][...] = jnp.zeros_like(r["vn"])

    @pl.when(jnp.logical_and(mixing, t == 0))
    def _():
        for c in range(CONV_TILES):
            s_ref[c, 0:HIST_ROWS, :] = jnp.zeros((HIST_ROWS, LANES), F32)

    @pl.when(jnp.logical_and(mixing, t > 0))
    def _():
        for c in range(CONV_TILES):
            s_ref[c, 0:HIST_ROWS, :] = s_ref[c, HALF:HALF + HIST_ROWS, :]

    bufs = (r["buf_a"], r["buf_b"])
    parity = lax.rem(i, 2)
    steady = jnp.logical_and(mixing, projecting)

    @pl.when(i == 0)
    def _():
        _mix_half_and_project(None, bufs[0], r, j)

    for par in (0, 1):
        @pl.when(jnp.logical_and(steady, parity == par))
        def _(par=par):
            _mix_half_and_project(bufs[1 - par], bufs[par], r, j)

    @pl.when(i == n_tiles)
    def _():
        _mix_half_and_project(bufs[1 - n_tiles % 2], None, r, j)

    @pl.when(jnp.logical_and(mixing, t == halves_per_seq - 1))
    def _():
        for c in range(CONV_TILES):
            r["conv"][0, :, c * LANES:(c + 1) * LANES] = s_ref[c, HALF:HALF + HIST_ROWS, :]

    @pl.when(jnp.logical_and(projecting, j == nj - 1))
    def _():
        _residual_rms_in_place(r["y"], r["x"], r["gpost"])


def _fused_mix_project(p2d, memk, memv, weights, w_out_b, x2d, *, batch, seq):
    nj = D_MODEL // FUSED_TN
    assert FUSED_TM == nj * HALF and seq % FUSED_TM == 0
    n_tiles = batch * seq // FUSED_TM
    halves_per_seq = seq // HALF
    mix_half = lambda i, j: jnp.minimum(i, n_tiles - 1) * nj + j
    proj_tile = lambda i: jnp.maximum(i - 1, 0)
    const = lambda a: pl.BlockSpec(a.shape, lambda i, j, _n=a.ndim: (0,) * _n)
    in_specs = [
        pl.BlockSpec((HALF, N_IN), lambda i, j: (mix_half(i, j), 0)),
        pl.BlockSpec((1, N_MEM, C_X), lambda i, j: (mix_half(i, j) // halves_per_seq, 0, 0)),
        pl.BlockSpec((1, N_MEM, C_X), lambda i, j: (mix_half(i, j) // halves_per_seq, 0, 0)),
    ]
    in_specs += [const(w) for w in weights]
    in_specs += [
        pl.BlockSpec((D_MODEL, FUSED_TN), lambda i, j: (0, j)),
        pl.BlockSpec((FUSED_TM, D_MODEL), lambda i, j: (proj_tile(i), 0)),
    ]
    return pl.pallas_call(
        functools.partial(_fused_kernel, n_tiles=n_tiles, halves_per_seq=halves_per_seq),
        grid=(n_tiles + 1, nj),
        in_specs=in_specs,
        out_specs=[
            pl.BlockSpec((FUSED_TM, D_MODEL), lambda i, j: (proj_tile(i), 0)),
            pl.BlockSpec((1, HIST_ROWS, C_CONV), lambda i, j: (mix_half(i, j) // halves_per_seq, 0, 0)),
        ],
        out_shape=[
            jax.ShapeDtypeStruct((batch * seq, D_MODEL), F32),
            jax.ShapeDtypeStruct((batch, HIST_ROWS, C_CONV), F32),
        ],
        scratch_shapes=[
            pltpu.VMEM((CONV_TILES, HIST_ROWS + HALF, LANES), F32),
            pltpu.VMEM((GMLP_CHUNK, C_GMLP), BF16),
            pltpu.VMEM((FUSED_TM, D_MODEL), BF16),
            pltpu.VMEM((FUSED_TM, D_MODEL), BF16),
        ],
        compiler_params=pltpu.CompilerParams(
            dimension_semantics=("arbitrary", "arbitrary"), vmem_limit_bytes=FUSED_VMEM_LIMIT),
        name="mixproj",
    )(p2d, memk, memv, *weights, w_out_b, x2d)


def _layer(x, memk, memv, mem_layer, hist, lw, *, tt, emit_vn, cast_weights):
    batch, seq, d = x.shape
    x2d = x.reshape(batch * seq, d)
    if cast_weights:
        p, w_in_b = _inproj(x2d, lw["g_pre"], lw["w_in"], tm=batch * seq, tn=512, emit_w_bf16=True)
    else:
        p, w_in_b = _inproj_stream(x2d, lw["g_pre"], lw["w_in"]), lw["w_in"]
    if hist is None and not emit_vn and not cast_weights and memk.ndim == 3 and memk.dtype == BF16:
        y, conv = _fused_mix_project(p, memk, memv, lw["mixer"] + (lw["g_post"],), lw["w_out"], x2d,
                                     batch=batch, seq=seq)
        return y.reshape(batch, seq, d), conv, w_in_b, lw["w_out"]
    outs = list(_mixer(p, memk, memv, mem_layer, hist, lw["mixer"], lw["w_out"] if cast_weights else None,
                       batch=batch, seq=seq, tt=tt, emit_vn=emit_vn))
    w_out_b = outs.pop() if cast_weights else lw["w_out"]
    y = _outproj(outs[0], w_out_b, x2d, lw["g_post"], tm=512, tn=512)
    return (y.reshape(batch, seq, d),) + tuple(outs[1:]) + (w_in_b, w_out_b)


def kernel(x_prompt, x_sample, mem_prompt, cache_mem_k, cache_mem_v, cache_conv, g_pre, w_in, conv_w, conv_b,
           ln_conv_g, ln_conv_b, ln_v_g, ln_v_b, w_spatial, b_spatial, g_mem, w_mk, w_mv, g_branch, w_out, g_post):
    depth = w_in.shape[0]
    yp, ys = x_prompt, x_sample
    batch = x_prompt.shape[0]
    mk_l, mv_l, cp_l, cs_l, gv_l = [], [], [], [], []
    for l in range(depth):
        row = lambda a: a[l].reshape(1, -1)
        lw = {
            "g_pre": row(g_pre),
            "w_in": w_in[l],
            "w_out": w_out[l],
            "g_post": row(g_post),
            "mixer": (
                conv_w[l].reshape(CONV_W, CONV_TILES, LANES).transpose(1, 0, 2),
                conv_b[l].reshape(CONV_TILES, 1, LANES),
                row(ln_conv_g), row(ln_conv_b), row(ln_v_g), row(ln_v_b),
                w_spatial[l], b_spatial[l].T, row(g_branch),
            ),
        }
        hist = jnp.pad(cache_conv[l], ((0, 0), (HIST_PAD, 0), (0, 0)))
        dec_batch = x_sample.shape[0]
        ys, cs, gv, w_in_b, w_out_b = _layer(ys, cache_mem_k[l].reshape(dec_batch, N_MEM, C_X),
                                             cache_mem_v[l].reshape(dec_batch, N_MEM, C_X), 0, hist, lw,
                                             tt=x_sample.shape[1], emit_vn=True, cast_weights=True)
        mk, mv, mk_b, mv_b = _memkv(mem_prompt.reshape(batch * N_MEM, D_MODEL), row(g_mem), w_mk[l], w_mv[l], tn=256)
        yp, cp, _, _ = _layer(yp, mk_b.reshape(batch, N_MEM, C_X), mv_b.reshape(batch, N_MEM, C_X), 0, None,
                              dict(lw, w_in=w_in_b, w_out=w_out_b), tt=GMLP_CHUNK, emit_vn=False, cast_weights=False)
        mk_l.append(mk.reshape(batch, N_MEM, X_HEADS, X_HEAD_DIM))
        mv_l.append(mv.reshape(batch, N_MEM, X_HEADS, X_HEAD_DIM))
        cp_l.append(cp[:, HIST_PAD:, :])
        cs_l.append(cs[:, HIST_PAD:, :])
        gv_l.append(gv)
    return (yp, ys, jnp.stack(mk_l), jnp.stack(mv_l), jnp.stack(cp_l), jnp.stack(cs_l), jnp.stack(gv_l))
```

```python
import functools

import jax
import jax.numpy as jnp
from jax import lax
from jax.experimental import pallas as pl
from jax.experimental.pallas import tpu as pltpu

EPS = 1e-6
D_MODEL = 4096
C_CONV = 1536
C_GMLP = 1536
C_X = 1024
CONV_W = 31
G_HEADS = 4
G_HEAD_DIM = C_GMLP // G_HEADS
X_HEADS = 4
X_HEAD_DIM = C_X // X_HEADS
N_MEM = 256
GMLP_CHUNK = 128
N_IN = 3 * C_CONV + 3 * C_GMLP + 2 * C_X

OFF_A, OFF_BG, OFF_GC = 0, C_CONV, 2 * C_CONV
OFF_U, OFF_V, OFF_GG = 3 * C_CONV, 3 * C_CONV + C_GMLP, 3 * C_CONV + 2 * C_GMLP
OFF_Q = 3 * C_CONV + 3 * C_GMLP
OFF_GX = OFF_Q + C_X

LANES = 128
SUBLANES = 8
CONV_TILES = C_CONV // LANES
HIST_ROWS = 32
HIST_PAD = HIST_ROWS - (CONV_W - 1)
CONV_CHAINS = 4
V7X_VMEM_BYTES = 64 * 1024 * 1024
VMEM_LIMIT_BYTES = V7X_VMEM_BYTES - 8 * 1024 * 1024
ROW_CHUNK = 64

BF16 = jnp.bfloat16
F32 = jnp.float32


def _rms_scale(x):
    return lax.rsqrt(jnp.mean(x * x, axis=-1, keepdims=True) + EPS)


def _layer_norm(x, g, b):
    xc = x - jnp.mean(x, axis=-1, keepdims=True)
    return xc * lax.rsqrt(jnp.mean(xc * xc, axis=-1, keepdims=True) + EPS) * g + b


def _sigmoid(x):
    return 0.5 * jnp.tanh(0.5 * x) + 0.5


def _silu(x):
    return x * _sigmoid(x)


def _exact_zero_of(x):
    bits = lax.bitcast_convert_type(x, jnp.uint32)
    bits = lax.shift_right_logical(lax.shift_right_logical(bits, jnp.uint32(16)), jnp.uint32(16))
    return lax.bitcast_convert_type(bits, F32)


def _row_groups(n_rows, group, body):
    def step(r, carry):
        base = pl.multiple_of(r * ROW_CHUNK, ROW_CHUNK)
        for s in range(ROW_CHUNK // group):
            body(pl.ds(base + s * group, group))
        return carry

    lax.fori_loop(0, n_rows // ROW_CHUNK, step, 0)


def _rms_rows_to_bf16(x_ref, g_ref, h_ref):
    def body(rows):
        x = x_ref[rows, :]
        h_ref[rows, :] = (x * _rms_scale(x) * g_ref[...]).astype(BF16)

    _row_groups(x_ref.shape[0], 2 * SUBLANES, body)


def _residual_rms_in_place(y_ref, x_ref, g_ref):
    def body(rows):
        out = y_ref[rows, :]
        y_ref[rows, :] = x_ref[rows, :] + out * _rms_scale(out) * g_ref[...]

    _row_groups(y_ref.shape[0], ROW_CHUNK, body)


def _inproj_kernel(x_ref, g_ref, w_ref, o_ref, *rest, emit_w_bf16):
    h_ref = rest[-1]

    @pl.when(pl.program_id(1) == 0)
    def _():
        _rms_rows_to_bf16(x_ref, g_ref, h_ref)

    w = w_ref[...].astype(BF16)
    if emit_w_bf16:
        rest[0][...] = w
    o_ref[...] = jnp.dot(h_ref[...], w, preferred_element_type=F32).astype(o_ref.dtype)


def _inproj(x2d, g, w, *, tm, tn, emit_w_bf16=False):
    m, d = x2d.shape
    n = w.shape[1]
    out_specs = [pl.BlockSpec((tm, tn), lambda i, j: (i, j))]
    out_shape = [jax.ShapeDtypeStruct((m, n), BF16)]
    x_mode = {}
    if emit_w_bf16:
        assert m == tm, "the bf16 weight copy is written once per column block, by the only row tile"
        out_specs.append(pl.BlockSpec((d, tn), lambda i, j: (0, j)))
        out_shape.append(jax.ShapeDtypeStruct((d, n), BF16))
        x_mode = dict(pipeline_mode=pl.Buffered(1))
    return pl.pallas_call(
        functools.partial(_inproj_kernel, emit_w_bf16=emit_w_bf16),
        grid=(m // tm, n // tn),
        in_specs=[
            pl.BlockSpec((tm, d), lambda i, j: (i, 0), **x_mode),
            pl.BlockSpec((1, d), lambda i, j: (0, 0)),
            pl.BlockSpec((d, tn), lambda i, j: (0, j)),
        ],
        out_specs=out_specs,
        out_shape=out_shape,
        scratch_shapes=[pltpu.VMEM((tm, d), BF16)],
        compiler_params=pltpu.CompilerParams(
            dimension_semantics=("parallel", "arbitrary"), vmem_limit_bytes=VMEM_LIMIT_BYTES),
        name="inproj_wcast" if emit_w_bf16 else "inproj",
    )(x2d, g, w)


STREAM_TM = 1024
STREAM_TN = 512
STREAM_CHUNK = 64


def _inproj_stream_kernel(x0_ref, xs_ref, g_ref, w_ref, o_ref, h_a, h_b, *, n_chunks):
    i, j = pl.program_id(0), pl.program_id(1)

    @pl.when(jnp.logical_and(i == 0, j == 0))
    def _():
        _rms_rows_to_bf16(x0_ref, g_ref, h_a)

    def step(h_cur, h_next):
        chunk = jnp.minimum(j, n_chunks - 1)
        group = 2 * SUBLANES
        for s in range(STREAM_CHUNK // group):
            x = xs_ref[s * group:(s + 1) * group, :]
            rows = pl.ds(pl.multiple_of(chunk * STREAM_CHUNK, STREAM_CHUNK) + s * group, group)
            h_next[rows, :] = (x * _rms_scale(x) * g_ref[...]).astype(BF16)
        o_ref[...] = jnp.dot(h_cur[...], w_ref[...], preferred_element_type=F32).astype(o_ref.dtype)

    parity = lax.rem(i, 2)

    @pl.when(parity == 0)
    def _():
        step(h_a, h_b)

    @pl.when(parity == 1)
    def _():
        step(h_b, h_a)


def _inproj_stream(x2d, g, w_bf16):
    m, d = x2d.shape
    n = w_bf16.shape[1]
    tm, tn = STREAM_TM, STREAM_TN
    n_tiles, n_chunks = m // tm, tm // STREAM_CHUNK
    assert m % tm == 0 and n % tn == 0 and n // tn >= n_chunks

    def next_rows(i, j):
        return (jnp.minimum(i + 1, n_tiles - 1) * n_chunks + jnp.minimum(j, n_chunks - 1), 0)

    return pl.pallas_call(
        functools.partial(_inproj_stream_kernel, n_chunks=n_chunks),
        grid=(n_tiles, n // tn),
        in_specs=[
            pl.BlockSpec((tm, d), lambda i, j: (0, 0), pipeline_mode=pl.Buffered(1)),
            pl.BlockSpec((STREAM_CHUNK, d), next_rows),
            pl.BlockSpec((1, d), lambda i, j: (0, 0)),
            pl.BlockSpec((d, tn), lambda i, j: (0, j)),
        ],
        out_specs=pl.BlockSpec((tm, tn), lambda i, j: (i, j)),
        out_shape=jax.ShapeDtypeStruct((m, n), BF16),
        scratch_shapes=[pltpu.VMEM((tm, d), BF16), pltpu.VMEM((tm, d), BF16)],
        compiler_params=pltpu.CompilerParams(
            dimension_semantics=("arbitrary", "arbitrary"), vmem_limit_bytes=VMEM_LIMIT_BYTES),
        name="inproj_stream",
    )(x2d, x2d, g, w_bf16)


def _memkv_kernel(mem_ref, g_ref, wk_ref, wv_ref, k_ref, v_ref, kb_ref, vb_ref, m_ref):
    @pl.when(pl.program_id(0) == 0)
    def _():
        _rms_rows_to_bf16(mem_ref, g_ref, m_ref)

    k = jnp.dot(m_ref[...], wk_ref[...].astype(BF16), preferred_element_type=F32)
    v = jnp.dot(m_ref[...], wv_ref[...].astype(BF16), preferred_element_type=F32)
    k_ref[...] = k
    v_ref[...] = v
    kb_ref[...] = k.astype(BF16)
    vb_ref[...] = v.astype(BF16)


def _memkv(mem2d, g, w_mk, w_mv, *, tn):
    m, d = mem2d.shape
    n = w_mk.shape[1]
    col = lambda j: (0, j)
    return pl.pallas_call(
        _memkv_kernel,
        grid=(n // tn,),
        in_specs=[
            pl.BlockSpec((m, d), lambda j: (0, 0)),
            pl.BlockSpec((1, d), lambda j: (0, 0)),
            pl.BlockSpec((d, tn), col),
            pl.BlockSpec((d, tn), col),
        ],
        out_specs=[pl.BlockSpec((m, tn), col)] * 4,
        out_shape=[jax.ShapeDtypeStruct((m, n), F32)] * 2 + [jax.ShapeDtypeStruct((m, n), BF16)] * 2,
        scratch_shapes=[pltpu.VMEM((m, d), BF16)],
        compiler_params=pltpu.CompilerParams(
            dimension_semantics=("arbitrary",), vmem_limit_bytes=VMEM_LIMIT_BYTES),
        name="memkv",
    )(mem2d, g, w_mk, w_mv)


def _mem_head(ref, h):
    return ref[0, :, h * X_HEAD_DIM:(h + 1) * X_HEAD_DIM]


_MIXER_INPUTS = ("p", "memk", "memv", "hist", "cw", "cb", "lcg", "lcb", "lvg", "lvb", "ws", "bs", "gb", "wout_f32")


def _mixer_kernel(*refs, tt, nt, from_cache, emit_vn, cast_wout):
    absent = set() if from_cache else {"hist"}
    absent |= set() if cast_wout else {"wout_f32"}
    names = [n for n in _MIXER_INPUTS if n not in absent]
    names += ["mixed", "conv"] + (["vn"] if emit_vn else []) + (["wout_b"] if cast_wout else []) + ["s"]
    r = dict(zip(names, refs))
    if cast_wout:
        r["wout_b"][...] = r["wout_f32"][...].astype(BF16)
    p_ref = r["p"]
    mixed_ref = r["mixed"]
    gb_ref = r["gb"]
    s_ref = r["s"]
    pf = lambda off, width: p_ref[:, off:off + width].astype(F32)

    t = pl.program_id(1)

    @pl.when(t == 0)
    def _():
        for c in range(CONV_TILES):
            if from_cache:
                s_ref[c, 0:HIST_ROWS, :] = r["hist"][0, :, c * LANES:(c + 1) * LANES]
            else:
                s_ref[c, 0:HIST_ROWS, :] = jnp.zeros((HIST_ROWS, LANES), F32)

    if nt > 1:
        @pl.when(t > 0)
        def _():
            for c in range(CONV_TILES):
                s_ref[c, 0:HIST_ROWS, :] = s_ref[c, tt:tt + HIST_ROWS, :]

    glu = pf(OFF_A, C_CONV) * _sigmoid(pf(OFF_BG, C_CONV))
    for c in range(CONV_TILES):
        s_ref[c, HIST_ROWS:HIST_ROWS + tt, :] = glu[:, c * LANES:(c + 1) * LANES]

    conv_tiles, chain_tails = [], []
    for c in range(CONV_TILES):
        w = r["cw"][c]
        taps = [jnp.broadcast_to(w[k:k + 1, :], (SUBLANES, LANES)) for k in range(CONV_W)]
        bias = jnp.broadcast_to(r["cb"][c], (SUBLANES, LANES))
        groups = []
        for g in range(tt // SUBLANES):
            acc = bias if len(chain_tails) < CONV_CHAINS else bias + _exact_zero_of(chain_tails[-CONV_CHAINS])
            for k in range(CONV_W):
                lo = g * SUBLANES + k + HIST_PAD
                acc = acc + taps[k] * s_ref[c, lo:lo + SUBLANES, :]
            groups.append(acc)
            chain_tails.append(acc)
        conv_tiles.append(jnp.concatenate(groups, axis=0))
    conv = jnp.concatenate(conv_tiles, axis=-1)

    @pl.when(t == nt - 1)
    def _():
        for c in range(CONV_TILES):
            r["conv"][0, :, c * LANES:(c + 1) * LANES] = s_ref[c, tt:tt + HIST_ROWS, :]

    yc = _silu(_layer_norm(conv, r["lcg"][...], r["lcb"][...])) * _silu(pf(OFF_GC, C_CONV))
    mixed_ref[:, 0:C_CONV] = (yc * _rms_scale(yc) * gb_ref[:, 0:C_CONV]).astype(BF16)

    vn = _layer_norm(pf(OFF_V, C_GMLP), r["lvg"][...], r["lvb"][...])
    if emit_vn:
        r["vn"][0] = vn
    vn_b = vn.astype(BF16)
    row = lax.broadcasted_iota(jnp.int32, (GMLP_CHUNK, GMLP_CHUNK), 0)
    col = lax.broadcasted_iota(jnp.int32, (GMLP_CHUNK, GMLP_CHUNK), 1)
    yg_heads = []
    for h in range(G_HEADS):
        lanes = slice(h * G_HEAD_DIM, (h + 1) * G_HEAD_DIM)
        wm = jnp.where(row >= col, r["ws"][h], 0.0).astype(BF16)[:tt, :tt]
        z = jnp.dot(wm, vn_b[:, lanes], preferred_element_type=F32) + r["bs"][0:tt, h:h + 1]
        yg_heads.append(pf(OFF_U + h * G_HEAD_DIM, G_HEAD_DIM) * z * _silu(pf(OFF_GG + h * G_HEAD_DIM, G_HEAD_DIM)))
    yg = jnp.concatenate(yg_heads, axis=-1)
    mixed_ref[:, C_CONV:C_CONV + C_GMLP] = (yg * _rms_scale(yg) * gb_ref[:, C_CONV:C_CONV + C_GMLP]).astype(BF16)

    scores = []
    for h in range(X_HEADS):
        q = (pf(OFF_Q + h * X_HEAD_DIM, X_HEAD_DIM) * (X_HEAD_DIM ** -0.5)).astype(BF16)
        k = _mem_head(r["memk"], h).astype(BF16)
        scores.append(lax.dot_general(q, k, (((1,), (1,)), ((), ())), preferred_element_type=F32))
    probs = []
    for s in scores:
        e = jnp.exp(s - jnp.max(s, axis=-1, keepdims=True))
        probs.append((e / jnp.sum(e, axis=-1, keepdims=True)).astype(BF16))
    yx_heads = []
    for h in range(X_HEADS):
        o = jnp.dot(probs[h], _mem_head(r["memv"], h).astype(BF16), preferred_element_type=F32)
        yx_heads.append(o * _silu(pf(OFF_GX + h * X_HEAD_DIM, X_HEAD_DIM)))
    yx = jnp.concatenate(yx_heads, axis=-1)
    mixed_ref[:, C_CONV + C_GMLP:D_MODEL] = (yx * _rms_scale(yx) * gb_ref[:, C_CONV + C_GMLP:D_MODEL]).astype(BF16)


def _mixer(p2d, memk, memv, mem_layer, hist, weights, wout_f32, *, batch, seq, tt, emit_vn):
    nt = seq // tt
    from_cache = hist is not None
    cast_wout = wout_f32 is not None
    const = lambda a: pl.BlockSpec(a.shape, lambda b, t, _n=a.ndim: (0,) * _n)
    in_specs = [pl.BlockSpec((tt, N_IN), lambda b, t: (b * nt + t, 0))]
    in_specs += [pl.BlockSpec((1, N_MEM, C_X), lambda b, t: (b, 0, 0))] * 2
    args = [p2d, memk, memv]
    if from_cache:
        in_specs.append(pl.BlockSpec((1, HIST_ROWS, C_CONV), lambda b, t: (b, 0, 0)))
        args.append(hist)
    in_specs += [const(w) for w in weights]
    args += list(weights)
    if cast_wout:
        w_rows = wout_f32.shape[0] // (batch * nt)
        assert w_rows * batch * nt == wout_f32.shape[0] and w_rows % 16 == 0
        w_block = pl.BlockSpec((w_rows, wout_f32.shape[1]), lambda b, t: (b * nt + t, 0))
        in_specs.append(w_block)
        args.append(wout_f32)
    out_specs = [
        pl.BlockSpec((tt, D_MODEL), lambda b, t: (b * nt + t, 0)),
        pl.BlockSpec((1, HIST_ROWS, C_CONV), lambda b, t: (b, 0, 0)),
    ]
    out_shape = [
        jax.ShapeDtypeStruct((batch * seq, D_MODEL), BF16),
        jax.ShapeDtypeStruct((batch, HIST_ROWS, C_CONV), F32),
    ]
    if emit_vn:
        out_specs.append(pl.BlockSpec((1, tt, C_GMLP), lambda b, t: (b, t, 0)))
        out_shape.append(jax.ShapeDtypeStruct((batch, seq, C_GMLP), F32))
    if cast_wout:
        out_specs.append(w_block)
        out_shape.append(jax.ShapeDtypeStruct(wout_f32.shape, BF16))
    return pl.pallas_call(
        functools.partial(_mixer_kernel, tt=tt, nt=nt, from_cache=from_cache, emit_vn=emit_vn, cast_wout=cast_wout),
        grid=(batch, nt),
        in_specs=in_specs,
        out_specs=out_specs,
        out_shape=out_shape,
        scratch_shapes=[pltpu.VMEM((CONV_TILES, HIST_ROWS + tt, LANES), F32)],
        compiler_params=pltpu.CompilerParams(
            dimension_semantics=("arbitrary", "arbitrary"), vmem_limit_bytes=VMEM_LIMIT_BYTES),
        name="mixer_cache" if from_cache else "mixer",
    )(*args)


def _outproj_kernel(m_ref, w_ref, x_ref, g_ref, y_ref, *, nj, tn):
    j = pl.program_id(1)
    y_ref[:, pl.ds(pl.multiple_of(j * tn, tn), tn)] = jnp.dot(m_ref[...], w_ref[...], preferred_element_type=F32)

    @pl.when(j == nj - 1)
    def _():
        _residual_rms_in_place(y_ref, x_ref, g_ref)


def _outproj(mixed, w_bf16, x2d, g, *, tm, tn):
    m, d = x2d.shape
    nj = d // tn
    return pl.pallas_call(
        functools.partial(_outproj_kernel, nj=nj, tn=tn),
        grid=(m // tm, nj),
        in_specs=[
            pl.BlockSpec((tm, d), lambda i, j: (i, 0)),
            pl.BlockSpec((d, tn), lambda i, j: (0, j)),
            pl.BlockSpec((tm, d), lambda i, j: (i, 0)),
            pl.BlockSpec((1, d), lambda i, j: (0, 0)),
        ],
        out_specs=pl.BlockSpec((tm, d), lambda i, j: (i, 0)),
        out_shape=jax.ShapeDtypeStruct((m, d), F32),
        compiler_params=pltpu.CompilerParams(
            dimension_semantics=("parallel", "arbitrary"), vmem_limit_bytes=VMEM_LIMIT_BYTES),
        name="outproj",
    )(mixed, w_bf16, x2d, g)


HALF = GMLP_CHUNK // 2
FUSED_TM = 512
FUSED_TN = 512
FUSED_SPLIT = 384
FUSED_CONV_CHAINS = 3
FUSED_VMEM_LIMIT = V7X_VMEM_BYTES - 3 * 1024 * 1024
_FUSED_INPUTS = ("p", "memk", "memv", "cw", "cb", "lcg", "lcb", "lvg", "lvb", "ws", "bs", "gb", "gpost", "wout", "x")


def _mix_half_and_project(proj_ref, mix_ref, acc_w, acc_r, r, j):
    rows = pl.ds(pl.multiple_of(j * HALF, HALF), HALF)

    if acc_r is not None:
        out = acc_r[rows, :]
        r["y"][...] = r["x"][...] + out * _rms_scale(out) * r["gpost"][...]

    def project(lo, hi):
        if proj_ref is not None:
            cols = pl.ds(pl.multiple_of(j * FUSED_TN, FUSED_TN), FUSED_TN)
            acc_w[lo:hi, cols] = jnp.dot(proj_ref[lo:hi, :], r["wout"][...], preferred_element_type=F32)

    if mix_ref is None:
        project(0, FUSED_TM)
        return

    p_ref, s_ref, gb_ref = r["p"], r["s"], r["gb"]
    pf = lambda off, width: p_ref[:, off:off + width].astype(F32)
    rows = pl.ds(pl.multiple_of(j * HALF, HALF), HALF)

    scores = []
    for h in range(X_HEADS):
        q = (pf(OFF_Q + h * X_HEAD_DIM, X_HEAD_DIM) * (X_HEAD_DIM ** -0.5)).astype(BF16)
        scores.append(lax.dot_general(q, _mem_head(r["memk"], h), (((1,), (1,)), ((), ())),
                                      preferred_element_type=F32))

    par = lax.rem(j, 2)
    half_rows = pl.ds(pl.multiple_of(par * HALF, HALF), HALF)
    vn = _layer_norm(pf(OFF_V, C_GMLP), r["lvg"][...], r["lvb"][...])
    r["vn"][half_rows, :] = vn.astype(BF16)
    vn_chunk = r["vn"][...]
    row = lax.broadcasted_iota(jnp.int32, (HALF, GMLP_CHUNK), 0) + par * HALF
    col = lax.broadcasted_iota(jnp.int32, (HALF, GMLP_CHUNK), 1)
    z_heads = []
    for h in range(G_HEADS):
        wm = jnp.where(row >= col, r["ws"][h, half_rows, :], 0.0).astype(BF16)
        z_heads.append(jnp.dot(wm, vn_chunk[:, h * G_HEAD_DIM:(h + 1) * G_HEAD_DIM], preferred_element_type=F32)
                       + r["bs"][half_rows, h:h + 1])

    project(0, FUSED_SPLIT)

    yx_heads = []
    for h in range(X_HEADS):
        e = jnp.exp(scores[h] - jnp.max(scores[h], axis=-1, keepdims=True))
        prob = (e / jnp.sum(e, axis=-1, keepdims=True)).astype(BF16)
        o = jnp.dot(prob, _mem_head(r["memv"], h), preferred_element_type=F32)
        yx_heads.append(o * _silu(pf(OFF_GX + h * X_HEAD_DIM, X_HEAD_DIM)))
    yx = jnp.concatenate(yx_heads, axis=-1)
    mix_ref[rows, C_CONV + C_GMLP:D_MODEL] = (yx * _rms_scale(yx) * gb_ref[:, C_CONV + C_GMLP:D_MODEL]).astype(BF16)

    yg = jnp.concatenate(
        [pf(OFF_U + h * G_HEAD_DIM, G_HEAD_DIM) * z_heads[h] * _silu(pf(OFF_GG + h * G_HEAD_DIM, G_HEAD_DIM))
         for h in range(G_HEADS)], axis=-1)
    mix_ref[rows, C_CONV:C_CONV + C_GMLP] = (yg * _rms_scale(yg) * gb_ref[:, C_CONV:C_CONV + C_GMLP]).astype(BF16)

    glu = pf(OFF_A, C_CONV) * _sigmoid(pf(OFF_BG, C_CONV))
    for c in range(CONV_TILES):
        s_ref[c, HIST_ROWS:HIST_ROWS + HALF, :] = glu[:, c * LANES:(c + 1) * LANES]
    conv_tiles, chain_tails = [], []
    for c in range(CONV_TILES):
        w = r["cw"][c]
        taps = [jnp.broadcast_to(w[k:k + 1, :], (SUBLANES, LANES)) for k in range(CONV_W)]
        bias = jnp.broadcast_to(r["cb"][c], (SUBLANES, LANES))
        groups = []
        for g in range(HALF // SUBLANES):
            acc = (bias if len(chain_tails) < FUSED_CONV_CHAINS
                   else bias + _exact_zero_of(chain_tails[-FUSED_CONV_CHAINS]))
            for k in range(CONV_W):
                lo = g * SUBLANES + k + HIST_PAD
                acc = acc + taps[k] * s_ref[c, lo:lo + SUBLANES, :]
            groups.append(acc)
            chain_tails.append(acc)
        conv_tiles.append(jnp.concatenate(groups, axis=0))
    conv = jnp.concatenate(conv_tiles, axis=-1)
    yc = _silu(_layer_norm(conv, r["lcg"][...], r["lcb"][...])) * _silu(pf(OFF_GC, C_CONV))
    mix_ref[rows, 0:C_CONV] = (yc * _rms_scale(yc) * gb_ref[:, 0:C_CONV]).astype(BF16)

    project(FUSED_SPLIT, FUSED_TM)


def _fused_kernel(*refs, n_tiles, halves_per_seq):
    names = list(_FUSED_INPUTS) + ["y", "conv", "s", "vn", "buf_a", "buf_b", "acc_a", "acc_b"]
    r = dict(zip(names, refs))
    s_ref = r["s"]
    nj = D_MODEL // FUSED_TN
    i = pl.program_id(0)
    j = pl.program_id(1)
    mixing = i < n_tiles
    t = lax.rem(jnp.minimum(i, n_tiles - 1) * nj + j, halves_per_seq)

    @pl.when(jnp.logical_and(i == 0, j == 0))
    def _():
        r["vn"][...] = jnp.zeros_like(r["vn"])

    @pl.when(jnp.logical_and(mixing, t == 0))
    def _():
        for c in range(CONV_TILES):
            s_ref[c, 0:HIST_ROWS, :] = jnp.zeros((HIST_ROWS, LANES), F32)

    @pl.when(jnp.logical_and(mixing, t > 0))
    def _():
        for c in range(CONV_TILES):
            s_ref[c, 0:HIST_ROWS, :] = s_ref[c, HALF:HALF + HIST_ROWS, :]

    bufs = (r["buf_a"], r["buf_b"])
    accs = (r["acc_a"], r["acc_b"])

    def stages(par, mix, project, finish):
        _mix_half_and_project(bufs[1 - par] if project else None, bufs[par] if mix else None,
                              accs[1 - par] if project else None, accs[par] if finish else None, r, j)

    @pl.when(i == 0)
    def _():
        stages(0, True, False, False)

    @pl.when(i == 1)
    def _():
        stages(1, True, True, False)

    parity = lax.rem(i, 2)
    steady = jnp.logical_and(i >= 2, mixing)
    for par in (0, 1):
        @pl.when(jnp.logical_and(steady, parity == par))
        def _(par=par):
            stages(par, True, True, True)

    @pl.when(i == n_tiles)
    def _():
        stages(n_tiles % 2, False, True, True)

    @pl.when(i == n_tiles + 1)
    def _():
        stages((n_tiles + 1) % 2, False, False, True)

    @pl.when(jnp.logical_and(mixing, t == halves_per_seq - 1))
    def _():
        for c in range(CONV_TILES):
            r["conv"][0, :, c * LANES:(c + 1) * LANES] = s_ref[c, HALF:HALF + HIST_ROWS, :]


def _fused_mix_project(p2d, memk, memv, weights, w_out_b, x2d, *, batch, seq):
    nj = D_MODEL // FUSED_TN
    assert FUSED_TM == nj * HALF and seq % FUSED_TM == 0
    n_tiles = batch * seq // FUSED_TM
    assert n_tiles >= 2
    halves_per_seq = seq // HALF
    mix_half = lambda i, j: jnp.minimum(i, n_tiles - 1) * nj + j

    def finish_half(i, j):
        return (jnp.where(i >= 2, (i - 2) * nj + j, 0), 0)

    const = lambda a: pl.BlockSpec(a.shape, lambda i, j, _n=a.ndim: (0,) * _n)
    in_specs = [
        pl.BlockSpec((HALF, N_IN), lambda i, j: (mix_half(i, j), 0)),
        pl.BlockSpec((1, N_MEM, C_X), lambda i, j: (mix_half(i, j) // halves_per_seq, 0, 0)),
        pl.BlockSpec((1, N_MEM, C_X), lambda i, j: (mix_half(i, j) // halves_per_seq, 0, 0)),
    ]
    in_specs += [const(w) for w in weights]
    in_specs += [
        pl.BlockSpec((D_MODEL, FUSED_TN), lambda i, j: (0, j)),
        pl.BlockSpec((HALF, D_MODEL), finish_half),
    ]
    return pl.pallas_call(
        functools.partial(_fused_kernel, n_tiles=n_tiles, halves_per_seq=halves_per_seq),
        grid=(n_tiles + 2, nj),
        in_specs=in_specs,
        out_specs=[
            pl.BlockSpec((HALF, D_MODEL), finish_half),
            pl.BlockSpec((1, HIST_ROWS, C_CONV), lambda i, j: (mix_half(i, j) // halves_per_seq, 0, 0)),
        ],
        out_shape=[
            jax.ShapeDtypeStruct((batch * seq, D_MODEL), F32),
            jax.ShapeDtypeStruct((batch, HIST_ROWS, C_CONV), F32),
        ],
        scratch_shapes=[
            pltpu.VMEM((CONV_TILES, HIST_ROWS + HALF, LANES), F32),
            pltpu.VMEM((GMLP_CHUNK, C_GMLP), BF16),
            pltpu.VMEM((FUSED_TM, D_MODEL), BF16),
            pltpu.VMEM((FUSED_TM, D_MODEL), BF16),
            pltpu.VMEM((FUSED_TM, D_MODEL), F32),
            pltpu.VMEM((FUSED_TM, D_MODEL), F32),
        ],
        compiler_params=pltpu.CompilerParams(
            dimension_semantics=("arbitrary", "arbitrary"), vmem_limit_bytes=VMEM_LIMIT_BYTES),
        name="mixproj",
    )(p2d, memk, memv, *weights, w_out_b, x2d)


def _layer(x, memk, memv, mem_layer, hist, lw, *, tt, emit_vn, cast_weights):
    batch, seq, d = x.shape
    x2d = x.reshape(batch * seq, d)
    if cast_weights:
        p, w_in_b = _inproj(x2d, lw["g_pre"], lw["w_in"], tm=batch * seq, tn=512, emit_w_bf16=True)
    else:
        p, w_in_b = _inproj_stream(x2d, lw["g_pre"], lw["w_in"]), lw["w_in"]
    if hist is None and not emit_vn and not cast_weights and memk.ndim == 3 and memk.dtype == BF16:
        y, conv = _fused_mix_project(p, memk, memv, lw["mixer"] + (lw["g_post"],), lw["w_out"], x2d,
                                     batch=batch, seq=seq)
        return y.reshape(batch, seq, d), conv, w_in_b, lw["w_out"]
    outs = list(_mixer(p, memk, memv, mem_layer, hist, lw["mixer"], lw["w_out"] if cast_weights else None,
                       batch=batch, seq=seq, tt=tt, emit_vn=emit_vn))
    w_out_b = outs.pop() if cast_weights else lw["w_out"]
    y = _outproj(outs[0], w_out_b, x2d, lw["g_post"], tm=512, tn=512)
    return (y.reshape(batch, seq, d),) + tuple(outs[1:]) + (w_in_b, w_out_b)


def kernel(x_prompt, x_sample, mem_prompt, cache_mem_k, cache_mem_v, cache_conv, g_pre, w_in, conv_w, conv_b,
           ln_conv_g, ln_conv_b, ln_v_g, ln_v_b, w_spatial, b_spatial, g_mem, w_mk, w_mv, g_branch, w_out, g_post):
    depth = w_in.shape[0]
    yp, ys = x_prompt, x_sample
    batch = x_prompt.shape[0]
    mk_l, mv_l, cp_l, cs_l, gv_l = [], [], [], [], []
    for l in range(depth):
        row = lambda a: a[l].reshape(1, -1)
        lw = {
            "g_pre": row(g_pre),
            "w_in": w_in[l],
            "w_out": w_out[l],
            "g_post": row(g_post),
            "mixer": (
                conv_w[l].reshape(CONV_W, CONV_TILES, LANES).transpose(1, 0, 2),
                conv_b[l].reshape(CONV_TILES, 1, LANES),
                row(ln_conv_g), row(ln_conv_b), row(ln_v_g), row(ln_v_b),
                w_spatial[l], b_spatial[l].T, row(g_branch),
            ),
        }
        hist = jnp.pad(cache_conv[l], ((0, 0), (HIST_PAD, 0), (0, 0)))
        dec_batch = x_sample.shape[0]
        ys, cs, gv, w_in_b, w_out_b = _layer(ys, cache_mem_k[l].reshape(dec_batch, N_MEM, C_X),
                                             cache_mem_v[l].reshape(dec_batch, N_MEM, C_X), 0, hist, lw,
                                             tt=x_sample.shape[1], emit_vn=True, cast_weights=True)
        mk, mv, mk_b, mv_b = _memkv(mem_prompt.reshape(batch * N_MEM, D_MODEL), row(g_mem), w_mk[l], w_mv[l], tn=256)
        yp, cp, _, _ = _layer(yp, mk_b.reshape(batch, N_MEM, C_X), mv_b.reshape(batch, N_MEM, C_X), 0, None,
                              dict(lw, w_in=w_in_b, w_out=w_out_b), tt=GMLP_CHUNK, emit_vn=False, cast_weights=False)
        mk_l.append(mk.reshape(batch, N_MEM, X_HEADS, X_HEAD_DIM))
        mv_l.append(mv.reshape(batch, N_MEM, X_HEADS, X_HEAD_DIM))
        cp_l.append(cp[:, HIST_PAD:, :])
        cs_l.append(cs[:, HIST_PAD:, :])
        gv_l.append(gv)
    return (yp, ys, jnp.stack(mk_l), jnp.stack(mv_l), jnp.stack(cp_l), jnp.stack(cs_l), jnp.stack(gv_l))
```
